```python
import math, functools
import jax, jax.numpy as jnp
from jax import lax
import numpy as np

D_MODEL = 1024
BATCH = 8
SEQ = 2048
DEPTH = 4
DEC_BATCH = 128
DEC_SEQ = 1
PAST_LEN = 8192
PAGE_SIZE = 128

MLA_HEADS = 8
QK_NOPE = 64
QK_ROPE = 32
V_HEAD = 64
Q_RANK = 384
KV_RANK = 256
ROPE_THETA = 10000.0
MLA_SCALE = 1.0 / math.sqrt(QK_NOPE + QK_ROPE)
Q_BLOCK = 128
HG_HEADS = 4
HG_DK = 128
HG_DV = 128
HG_WIDTH = HG_HEADS * HG_DK
HG_CHUNK = 64
CONV_CH = 512
CONV_WIDTH = 31
D_FF = 2816
N_BRANCH = 3
EPS = 1e-6

IN_SIZES = (Q_RANK, KV_RANK, QK_ROPE, HG_WIDTH, HG_WIDTH, HG_HEADS * HG_DV,
            HG_HEADS * HG_DV, 2 * CONV_CH, N_BRANCH * D_MODEL)
D_IN = Q_RANK + KV_RANK + QK_ROPE + 2 * HG_WIDTH + 2 * HG_HEADS * HG_DV + 2 * CONV_CH + N_BRANCH * D_MODEL

kernel_name = 'hybrid_mla_hgrn2_conformer_step'

F32 = jnp.float32


def rmsnorm(x, g):
    xf = x.astype(F32)
    y = xf * lax.rsqrt(jnp.mean(xf * xf, axis=-1, keepdims=True) + EPS)
    return (y * g.astype(F32)).astype(x.dtype)


def layernorm(x, g, b):
    xf = x.astype(F32)
    mu = jnp.mean(xf, axis=-1, keepdims=True)
    var = jnp.mean(jnp.square(xf - mu), axis=-1, keepdims=True)
    y = (xf - mu) * lax.rsqrt(var + EPS) * g.astype(F32) + b.astype(F32)
    return y.astype(x.dtype)


def swiglu(x, w_gate, w_up, w_down):
    return (jax.nn.silu(x @ w_gate) * (x @ w_up)) @ w_down


def rope_angles(pos):
    half = QK_ROPE // 2
    inv = ROPE_THETA ** (-jnp.arange(half, dtype=F32) / half)
    ang = pos.astype(F32)[:, None] * inv[None, :]
    return jnp.cos(ang)[:, None, :], jnp.sin(ang)[:, None, :]


def apply_rope(x, cos, sin):
    x1, x2 = jnp.split(x.astype(F32), 2, axis=-1)
    out = jnp.concatenate([x1 * cos - x2 * sin, x1 * sin + x2 * cos], axis=-1)
    return out.astype(x.dtype)


def split_in(z):
    cuts, acc = [], 0
    for s in IN_SIZES[:-1]:
        acc += s
        cuts.append(acc)
    return jnp.split(z, cuts, axis=-1)


def mla_prompt(q_nope, q_pe, c_kv, k_pe, w_uk, w_uv):
    B, S = c_kv.shape[:2]
    k_nope = jnp.einsum('bsr,rhd->bshd', c_kv, w_uk)
    v = jnp.einsum('bsr,rhd->bshd', c_kv, w_uv)
    nb = S // Q_BLOCK
    qn = q_nope.reshape(B, nb, Q_BLOCK, MLA_HEADS, QK_NOPE).transpose(1, 0, 2, 3, 4)
    qp = q_pe.reshape(B, nb, Q_BLOCK, MLA_HEADS, QK_ROPE).transpose(1, 0, 2, 3, 4)
    key_pos = jnp.arange(S)

    def block(args):
        qn_b, qp_b, i = args
        s = (jnp.einsum('bqhd,bkhd->bhqk', qn_b, k_nope)
             + jnp.einsum('bqhd,bkd->bhqk', qp_b, k_pe)).astype(F32) * MLA_SCALE
        q_pos = i * Q_BLOCK + jnp.arange(Q_BLOCK)
        s = jnp.where(key_pos[None, :] <= q_pos[:, None], s, -jnp.inf)
        p = jax.nn.softmax(s, axis=-1).astype(v.dtype)
        return jnp.einsum('bhqk,bkhd->bqhd', p, v)

    o = lax.map(block, (qn, qp, jnp.arange(nb)))
    return o.transpose(1, 0, 2, 3, 4).reshape(B, S, MLA_HEADS * V_HEAD)


def mla_sample(q_nope, q_pe, c_kv, k_pe, w_uk, w_uv, lat_past, pe_past):
    B, T = c_kv.shape[:2]
    P = lat_past.shape[1]
    q_lat = jnp.einsum('bthd,rhd->bthr', q_nope, w_uk)
    s_past = (jnp.einsum('bthr,bpr->bhtp', q_lat, lat_past)
              + jnp.einsum('bthd,bpd->bhtp', q_pe, pe_past)).astype(F32) * MLA_SCALE
    s_new = (jnp.einsum('bthr,bsr->bhts', q_lat, c_kv)
             + jnp.einsum('bthd,bsd->bhts', q_pe, k_pe)).astype(F32) * MLA_SCALE
    causal = jnp.arange(T)[None, :] <= jnp.arange(T)[:, None]
    s_new = jnp.where(causal, s_new, -jnp.inf)
    p = jax.nn.softmax(jnp.concatenate([s_past, s_new], axis=-1), axis=-1).astype(lat_past.dtype)
    o_lat = (jnp.einsum('bhtp,bpr->bthr', p[..., :P], lat_past)
             + jnp.einsum('bhts,bsr->bthr', p[..., P:], c_kv))
    o = jnp.einsum('bthr,rhd->bthd', o_lat, w_uv)
    return o.reshape(B, T, MLA_HEADS * V_HEAD)


def hgrn_chunk(S0, q, k, v, logf):
    L = q.shape[2]
    b = jnp.cumsum(logf, axis=2)
    causal = jnp.arange(L)[:, None] >= jnp.arange(L)[None, :]
    diff = b[:, :, :, None, :] - b[:, :, None, :, :]
    decay = jnp.exp(jnp.where(causal[:, :, None], diff, -jnp.inf))
    A = jnp.einsum('bhtk,bhsk,bhtsk->bhts', q, k, decay)
    o = jnp.einsum('bhts,bhsv->bhtv', A, v) + jnp.einsum('bhtk,bhkv->bhtv', q * jnp.exp(b), S0)
    b_last = b[:, :, -1:, :]
    k_dec = k * jnp.exp(b_last - b)
    S = jnp.exp(b_last[:, :, 0, :])[..., None] * S0 + jnp.einsum('bhsk,bhsv->bhkv', k_dec, v)
    return S, o


def hgrn_branch(hq, hf, hi, hgate, lb, S0, chunk, norm_g, w_o):
    B, T, _ = hq.shape
    nc = T // chunk
    f = lb + (1.0 - lb) * jax.nn.sigmoid(hf.astype(F32))

    def heads(z, d):
        return z.reshape(B, nc, chunk, HG_HEADS, d).transpose(1, 0, 3, 2, 4)

    q = heads(jax.nn.silu(hq.astype(F32)) * (HG_DK ** -0.5), HG_DK)
    k = heads(1.0 - f, HG_DK)
    logf = heads(jnp.log(f), HG_DK)
    v = heads(hi.astype(F32), HG_DV)
    S_final, o = lax.scan(lambda S, xs: hgrn_chunk(S, *xs), S0.astype(F32), (q, k, v, logf))
    o = o.transpose(1, 0, 3, 2, 4).reshape(B, T, HG_HEADS, HG_DV)
    o = rmsnorm(o, norm_g).reshape(B, T, HG_HEADS * HG_DV) * jax.nn.silu(hgate.astype(F32))
    return o.astype(hq.dtype) @ w_o, S_final


def conv_branch(glu, conv_prev, w_dw, b_dw, ln_g, ln_b, w_o):
    u = glu[..., :CONV_CH] * jax.nn.sigmoid(glu[..., CONV_CH:])
    up = jnp.concatenate([conv_prev.astype(u.dtype), u], axis=1)
    y = lax.conv_general_dilated(up, w_dw[:, None, :], window_strides=(1,), padding='VALID',
                                 dimension_numbers=('NWC', 'WIO', 'NWC'),
                                 feature_group_count=CONV_CH) + b_dw
    y = jax.nn.silu(layernorm(y, ln_g, ln_b))
    return y @ w_o, up[:, -(CONV_WIDTH - 1):]


def token_mixing(h, cos, sin, lb, hg_S0, hg_chunk, conv_prev, attend, p):
    B, T, _ = h.shape
    c_q, c_kv_raw, k_pe_raw, hq, hf, hi, hgate, glu, gates = split_in(h @ p['w_in'])
    q = (rmsnorm(c_q, p['mla_q_norm']) @ p['mla_w_uq']).reshape(B, T, MLA_HEADS, QK_NOPE + QK_ROPE)
    q_nope = q[..., :QK_NOPE]
    q_pe = apply_rope(q[..., QK_NOPE:], cos, sin)
    c_kv = rmsnorm(c_kv_raw, p['mla_kv_norm'])
    k_pe = apply_rope(k_pe_raw[:, :, None, :], cos, sin)[:, :, 0]
    w_ukv = p['mla_w_ukv'].reshape(KV_RANK, MLA_HEADS, QK_NOPE + V_HEAD)
    y_mla = attend(q_nope, q_pe, c_kv, k_pe, w_ukv[..., :QK_NOPE], w_ukv[..., QK_NOPE:]) @ p['mla_w_o']
    y_hg, S_new = hgrn_branch(hq, hf, hi, hgate, lb, hg_S0, hg_chunk, p['hg_norm'], p['hg_w_o'])
    y_conv, conv_new = conv_branch(glu, conv_prev, p['conv_w'], p['conv_b'], p['conv_ln_g'],
                                   p['conv_ln_b'], p['conv_w_o'])
    g = jax.nn.sigmoid(gates).reshape(B, T, N_BRANCH, D_MODEL)
    merged = g[:, :, 0] * y_mla + g[:, :, 1] * y_hg + g[:, :, 2] * y_conv
    return merged @ p['w_out'], (c_kv, k_pe, S_new, conv_new)


def trunk_layer(x, cos, sin, lb, hg_S0, hg_chunk, conv_prev, attend, p):
    x = x + 0.5 * swiglu(rmsnorm(x, p['norm_ffa']), p['w_ffa_gate'], p['w_ffa_up'], p['w_ffa_down'])
    y, new_state = token_mixing(rmsnorm(x, p['norm_mix']), cos, sin, lb, hg_S0, hg_chunk,
                                conv_prev, attend, p)
    x = x + y
    x = x + 0.5 * swiglu(rmsnorm(x, p['norm_ffb']), p['w_ffb_gate'], p['w_ffb_up'], p['w_ffb_down'])
    return x, new_state


def setup_inputs(seed: int = 0) -> dict:
    key = jax.random.key(seed)
    keys = jax.random.split(key, 48)
    ctr = [0]

    def nk():
        k = keys[ctr[0]]
        ctr[0] += 1
        return k

    def w(shape, fan_in):
        return jax.random.normal(nk(), shape, F32) * (fan_in ** -0.5)

    def gain(shape):
        return 1.0 + 0.02 * jax.random.normal(nk(), shape, F32)

    def small(shape):
        return 0.02 * jax.random.normal(nk(), shape, F32)

    n_pages = PAST_LEN // PAGE_SIZE
    n_used = DEC_BATCH * n_pages
    n_pool = n_used + n_used // 4
    page_table = jax.random.permutation(nk(), n_pool)[:n_used].reshape(DEC_BATCH, n_pages).astype(jnp.int32)
    L = DEPTH
    return {
        'x_prompt': jax.random.normal(nk(), (BATCH, SEQ, D_MODEL), F32),
        'x_sample': jax.random.normal(nk(), (DEC_BATCH, DEC_SEQ, D_MODEL), F32),
        'cache_kv_latent': jax.random.normal(nk(), (L, n_pool, PAGE_SIZE, KV_RANK), F32),
        'cache_k_rope': jax.random.normal(nk(), (L, n_pool, PAGE_SIZE, QK_ROPE), F32),
        'state_hgrn': 0.3 * jax.random.normal(nk(), (L, DEC_BATCH, HG_HEADS, HG_DK, HG_DV), F32),
        'state_conv': 0.5 * jax.random.normal(nk(), (L, DEC_BATCH, CONV_WIDTH - 1, CONV_CH), F32),
        'page_table': page_table,
        'norm_ffa': gain((L, D_MODEL)),
        'w_ffa_gate': w((L, D_MODEL, D_FF), D_MODEL),
        'w_ffa_up': w((L, D_MODEL, D_FF), D_MODEL),
        'w_ffa_down': w((L, D_FF, D_MODEL), D_FF),
        'norm_mix': gain((L, D_MODEL)),
        'w_in': w((L, D_MODEL, D_IN), D_MODEL),
        'mla_q_norm': gain((L, Q_RANK)),
        'mla_w_uq': w((L, Q_RANK, MLA_HEADS * (QK_NOPE + QK_ROPE)), Q_RANK),
        'mla_kv_norm': gain((L, KV_RANK)),
        'mla_w_ukv': w((L, KV_RANK, MLA_HEADS * (QK_NOPE + V_HEAD)), KV_RANK),
        'mla_w_o': w((L, MLA_HEADS * V_HEAD, D_MODEL), MLA_HEADS * V_HEAD),
        'hg_lower_bounds': 0.5 * jax.random.normal(nk(), (L, HG_WIDTH), F32),
        'hg_norm': gain((L, HG_DV)),
        'hg_w_o': w((L, HG_HEADS * HG_DV, D_MODEL), HG_HEADS * HG_DV),
        'conv_w': w((L, CONV_WIDTH, CONV_CH), CONV_WIDTH),
        'conv_b': small((L, CONV_CH)),
        'conv_ln_g': gain((L, CONV_CH)),
        'conv_ln_b': small((L, CONV_CH)),
        'conv_w_o': w((L, CONV_CH, D_MODEL), CONV_CH),
        'w_out': w((L, D_MODEL, D_MODEL), D_MODEL),
        'norm_ffb': gain((L, D_MODEL)),
        'w_ffb_gate': w((L, D_MODEL, D_FF), D_MODEL),
        'w_ffb_up': w((L, D_MODEL, D_FF), D_MODEL),
        'w_ffb_down': w((L, D_FF, D_MODEL), D_FF),
        'norm_final': gain((D_MODEL,)),
    }


def reference(x_prompt, x_sample, cache_kv_latent, cache_k_rope, state_hgrn, state_conv, page_table,
              norm_ffa, w_ffa_gate, w_ffa_up, w_ffa_down, norm_mix, w_in,
              mla_q_norm, mla_w_uq, mla_kv_norm, mla_w_ukv, mla_w_o,
              hg_lower_bounds, hg_norm, hg_w_o,
              conv_w, conv_b, conv_ln_g, conv_ln_b, conv_w_o, w_out,
              norm_ffb, w_ffb_gate, w_ffb_up, w_ffb_down, norm_final):
    lb_all = jnp.cumsum(jax.nn.softmax(hg_lower_bounds.astype(F32), axis=0), axis=0)
    lb_all = lb_all - lb_all[0:1]
    cos_p, sin_p = rope_angles(jnp.arange(SEQ))
    cos_s, sin_s = rope_angles(PAST_LEN + jnp.arange(DEC_SEQ))
    hg_zero = jnp.zeros((BATCH, HG_HEADS, HG_DK, HG_DV), F32)
    conv_zero = jnp.zeros((BATCH, CONV_WIDTH - 1, CONV_CH), x_prompt.dtype)

    xp, xs = x_prompt, x_sample
    lat_p, pe_p, hg_p, cv_p = [], [], [], []
    lat_s, pe_s, hg_s, cv_s = [], [], [], []
    for l in range(DEPTH):
        p = dict(norm_ffa=norm_ffa[l], w_ffa_gate=w_ffa_gate[l], w_ffa_up=w_ffa_up[l],
                 w_ffa_down=w_ffa_down[l], norm_mix=norm_mix[l], w_in=w_in[l],
                 mla_q_norm=mla_q_norm[l], mla_w_uq=mla_w_uq[l], mla_kv_norm=mla_kv_norm[l],
                 mla_w_ukv=mla_w_ukv[l], mla_w_o=mla_w_o[l], hg_norm=hg_norm[l], hg_w_o=hg_w_o[l],
                 conv_w=conv_w[l], conv_b=conv_b[l], conv_ln_g=conv_ln_g[l], conv_ln_b=conv_ln_b[l],
                 conv_w_o=conv_w_o[l], w_out=w_out[l], norm_ffb=norm_ffb[l],
                 w_ffb_gate=w_ffb_gate[l], w_ffb_up=w_ffb_up[l], w_ffb_down=w_ffb_down[l])
        lb = lb_all[l]
        xp, (c1, k1, s1, v1) = trunk_layer(xp, cos_p, sin_p, lb, hg_zero, HG_CHUNK, conv_zero,
                                           mla_prompt, p)
        lat_p.append(c1); pe_p.append(k1); hg_p.append(s1); cv_p.append(v1)
        lat_past = cache_kv_latent[l, page_table].reshape(DEC_BATCH, -1, KV_RANK)
        pe_past = cache_k_rope[l, page_table].reshape(DEC_BATCH, -1, QK_ROPE)
        attend_s = functools.partial(mla_sample, lat_past=lat_past, pe_past=pe_past)
        xs, (c2, k2, s2, v2) = trunk_layer(xs, cos_s, sin_s, lb, state_hgrn[l], DEC_SEQ,
                                           state_conv[l], attend_s, p)
        lat_s.append(c2); pe_s.append(k2); hg_s.append(s2); cv_s.append(v2)

    y_prompt = rmsnorm(xp, norm_final)
    y_sample = rmsnorm(xs, norm_final)
    return (y_prompt, y_sample,
            jnp.stack(lat_p), jnp.stack(pe_p), jnp.stack(hg_p), jnp.stack(cv_p),
            jnp.stack(lat_s), jnp.stack(pe_s), jnp.stack(hg_s), jnp.stack(cv_s))
```

```python
import functools
import math

import jax
import jax.numpy as jnp
from jax import lax
from jax.experimental import pallas as pl
from jax.experimental.pallas import tpu as pltpu

F32 = jnp.float32
BF16 = jnp.bfloat16

D_MODEL = 1024
DEPTH = 4
PAGE_SIZE = 128
MLA_HEADS = 8
QK_NOPE = 64
QK_ROPE = 32
V_HEAD = 64
Q_RANK = 384
KV_RANK = 256
ROPE_THETA = 10000.0
MLA_SCALE = 1.0 / math.sqrt(QK_NOPE + QK_ROPE)
HG_HEADS = 4
HG_DK = 128
HG_DV = 128
HG_WIDTH = HG_HEADS * HG_DK
CONV_CH = 512
CONV_WIDTH = 31
EPS = 1e-6

LANES = 128
SUBLANES = 8
HEAD_PAD = 128
MLA_SEG = Q_RANK + KV_RANK + 2 * QK_ROPE + 64
HG_SEG = 4 * HG_WIDTH
GLU_SEG = 2 * CONV_CH
GATE_SEG = 3 * D_MODEL
Z_WIDTH = HG_SEG + GLU_SEG + GATE_SEG + MLA_SEG
VMEM_LIMIT = 56 * 1024 * 1024
NEG = -1e30
HG_CHUNK = 128
CONV_PAD = 32


def _cp(*sem):
    return pltpu.CompilerParams(dimension_semantics=sem, vmem_limit_bytes=VMEM_LIMIT)


def _rms(x, g):
    ms = jnp.mean(x * x, axis=-1, keepdims=True)
    return x * lax.rsqrt(ms + EPS) * g


def _silu(x):
    return x * jax.nn.sigmoid(x)


def _dot(a, b):
    return jnp.dot(a, b, preferred_element_type=F32)


def _dot_nt(a, b):
    return lax.dot_general(a, b, (((1,), (1,)), ((), ())), preferred_element_type=F32)


def _ffn_kernel(x_ref, g_ref, wg_ref, wu_ref, wd_ref, o_ref, h_ref, acc_ref):
    j = pl.program_id(1)

    @pl.when(j == 0)
    def _():
        h_ref[...] = _rms(x_ref[...], g_ref[...]).astype(BF16)
        acc_ref[...] = jnp.zeros_like(acc_ref)

    h = h_ref[...]
    g = _dot(h, wg_ref[...])
    u = _dot(h, wu_ref[...])
    a = (_silu(g) * u).astype(BF16)
    acc_ref[...] += _dot(a, wd_ref[...])

    @pl.when(j == pl.num_programs(1) - 1)
    def _():
        o_ref[...] = x_ref[...] + 0.5 * acc_ref[...]


def _ffn(x, g, wg, wu, wd, tm):
    T, D = x.shape
    FF = wg.shape[1]
    tf = 256
    return pl.pallas_call(
        _ffn_kernel,
        grid=(T // tm, FF // tf),
        in_specs=[pl.BlockSpec((tm, D), lambda i, j: (i, 0)),
                  pl.BlockSpec((1, D), lambda i, j: (0, 0)),
                  pl.BlockSpec((D, tf), lambda i, j: (0, j)),
                  pl.BlockSpec((D, tf), lambda i, j: (0, j)),
                  pl.BlockSpec((tf, D), lambda i, j: (j, 0))],
        out_specs=pl.BlockSpec((tm, D), lambda i, j: (i, 0)),
        out_shape=jax.ShapeDtypeStruct((T, D), F32),
        scratch_shapes=[pltpu.VMEM((tm, D), BF16), pltpu.VMEM((tm, D), F32)],
        compiler_params=_cp("parallel", "arbitrary"),
        name="ffn",
    )(x, g, wg, wu, wd)


def _inproj_kernel(x_ref, g_ref, w_ref, o_ref, h_ref):
    @pl.when(pl.program_id(1) == 0)
    def _():
        h_ref[...] = _rms(x_ref[...], g_ref[...]).astype(BF16)

    o_ref[...] = _dot(h_ref[...], w_ref[...])


def _inproj(x, g, w, tm):
    T, D = x.shape
    N = w.shape[1]
    tn = 1152
    return pl.pallas_call(
        _inproj_kernel,
        grid=(T // tm, N // tn),
        in_specs=[pl.BlockSpec((tm, D), lambda i, j: (i, 0)),
                  pl.BlockSpec((1, D), lambda i, j: (0, 0)),
                  pl.BlockSpec((D, tn), lambda i, j: (0, j))],
        out_specs=pl.BlockSpec((tm, tn), lambda i, j: (i, j)),
        out_shape=jax.ShapeDtypeStruct((T, N), F32),
        scratch_shapes=[pltpu.VMEM((tm, D), BF16)],
        compiler_params=_cp("parallel", "arbitrary"),
        name="inproj",
    )(x, g, w)


def _norm_kernel(x_ref, g_ref, o_ref):
    o_ref[...] = _rms(x_ref[...], g_ref[...])


def _final_norm(x, g, tm):
    T, D = x.shape
    return pl.pallas_call(
        _norm_kernel,
        grid=(T // tm,),
        in_specs=[pl.BlockSpec((tm, D), lambda i: (i, 0)), pl.BlockSpec((1, D), lambda i: (0, 0))],
        out_specs=pl.BlockSpec((tm, D), lambda i: (i, 0)),
        out_shape=jax.ShapeDtypeStruct((T, D), F32),
        compiler_params=_cp("parallel"),
        name="final_norm",
    )(x, g)


def _lb_kernel(x_ref, o_ref):
    x = x_ref[...]
    e = jnp.exp(x - jnp.max(x, axis=0, keepdims=True))
    p = e / jnp.sum(e, axis=0, keepdims=True)
    rows = [p[0:1]]
    for l in range(1, x.shape[0]):
        rows.append(rows[-1] + p[l:l + 1])
    c = jnp.concatenate(rows, axis=0)
    o_ref[...] = c - c[0:1]


def _lower_bounds(x):
    return pl.pallas_call(_lb_kernel, out_shape=jax.ShapeDtypeStruct(x.shape, F32), name="hg_lower_bounds")(x)


def _mla_prep_kernel(z_ref, qn_ref, kvn_ref, wqa_ref, wqb_ref, cq_ref, sq_ref, wk_ref, wv_ref, pk_ref,
                     ck_ref, sk_ref, q_ref, lat_ref, pe_ref, k_ref, v_ref):
    z = z_ref[...]
    cq = z[:, :Q_RANK]
    ckv = z[:, Q_RANK:Q_RANK + KV_RANK]
    o = Q_RANK + KV_RANK
    kpe = z[:, o:o + QK_ROPE]
    kpes = z[:, o + QK_ROPE:o + 2 * QK_ROPE]
    qn = _rms(cq, qn_ref[...]).astype(BF16)
    qa = _dot(qn, wqa_ref[...])
    qb = _dot(qn, wqb_ref[...])
    cs = cq_ref[...]
    sn = sq_ref[...]
    for h in range(MLA_HEADS):
        sl = slice(h * HEAD_PAD, (h + 1) * HEAD_PAD)
        q_ref[:, sl] = (qa[:, sl] * cs + qb[:, sl] * sn).astype(BF16)
    lat = _rms(ckv, kvn_ref[...])
    lat_ref[...] = lat
    per = kpe * ck_ref[...] + kpes * sk_ref[...]
    pe_ref[...] = per
    latb = lat.astype(BF16)
    k_ref[...] = (_dot(latb, wk_ref[...]) + _dot(per.astype(BF16), pk_ref[...])).astype(BF16)
    v_ref[...] = _dot(latb, wv_ref[...]).astype(BF16)


def _mla_prep(z, S, qn, kvn, wqa, wqb, cosq, sinq, wk, wv, pk, cosk, sink, tm):
    T = z.shape[0]
    nb = S // tm
    zblk = (Z_WIDTH - MLA_SEG) // MLA_SEG
    const = lambda i: (0, 0)
    pos = lambda i: (i % nb, 0)
    H = MLA_HEADS
    return pl.pallas_call(
        _mla_prep_kernel,
        grid=(T // tm,),
        in_specs=[pl.BlockSpec((tm, MLA_SEG), lambda i: (i, zblk)),
                  pl.BlockSpec((1, Q_RANK), const), pl.BlockSpec((1, KV_RANK), const),
                  pl.BlockSpec((Q_RANK, H * HEAD_PAD), const), pl.BlockSpec((Q_RANK, H * HEAD_PAD), const),
                  pl.BlockSpec((tm, HEAD_PAD), pos), pl.BlockSpec((tm, HEAD_PAD), pos),
                  pl.BlockSpec((KV_RANK, H * HEAD_PAD), const), pl.BlockSpec((KV_RANK, H * V_HEAD), const),
                  pl.BlockSpec((QK_ROPE, H * HEAD_PAD), const),
                  pl.BlockSpec((tm, QK_ROPE), pos), pl.BlockSpec((tm, QK_ROPE), pos)],
        out_specs=[pl.BlockSpec((tm, H * HEAD_PAD), lambda i: (i, 0)),
                   pl.BlockSpec((tm, KV_RANK), lambda i: (i, 0)),
                   pl.BlockSpec((tm, QK_ROPE), lambda i: (i, 0)),
                   pl.BlockSpec((tm, H * HEAD_PAD), lambda i: (i, 0)),
                   pl.BlockSpec((tm, H * V_HEAD), lambda i: (i, 0))],
        out_shape=[jax.ShapeDtypeStruct((T, H * HEAD_PAD), BF16),
                   jax.ShapeDtypeStruct((T, KV_RANK), F32),
                   jax.ShapeDtypeStruct((T, QK_ROPE), F32),
                   jax.ShapeDtypeStruct((T, H * HEAD_PAD), BF16),
                   jax.ShapeDtypeStruct((T, H * V_HEAD), BF16)],
        compiler_params=_cp("parallel"),
        name="mla_prep",
    )(z, qn, kvn, wqa, wqb, cosq, sinq, wk, wv, pk, cosk, sink)


def _attn_kernel(q_ref, k_ref, v_ref, o_ref, *, tq):
    qi = pl.program_id(1)
    row = lax.broadcasted_iota(jnp.int32, (tq, tq), 0)
    col = lax.broadcasted_iota(jnp.int32, (tq, tq), 1)
    causal = col <= row
    lane = lax.broadcasted_iota(jnp.int32, (tq, LANES), 1)
    outs = []
    for h in range(MLA_HEADS):
        qs = slice(h * HEAD_PAD, (h + 1) * HEAD_PAD)
        vs = slice((h // 2) * LANES, (h // 2 + 1) * LANES)
        q = q_ref[:, qs]

        def step(off, carry, masked, qs=qs, vs=vs, q=q):
            m, l, acc = carry
            k = k_ref[pl.ds(off, tq), qs]
            v = v_ref[pl.ds(off, tq), vs]
            s = _dot_nt(q, k)
            if masked:
                s = jnp.where(causal, s, NEG)
            m_new = jnp.maximum(m, jnp.max(s, axis=-1, keepdims=True))
            a = jnp.exp(m - m_new)
            p = jnp.exp(s - m_new)
            l = a * l + jnp.sum(p, axis=-1, keepdims=True)
            acc = a * acc + _dot(p.astype(BF16), v)
            return m_new, l, acc

        init = (jnp.full((tq, 1), NEG, F32), jnp.zeros((tq, 1), F32), jnp.zeros((tq, LANES), F32))
        carry = lax.fori_loop(0, qi, lambda j, c, step=step: step(pl.multiple_of(j * tq, tq), c, False), init)
        _, l, acc = step(pl.multiple_of(qi * tq, tq), carry, True)
        outs.append(acc / l)
    for p in range(MLA_HEADS // 2):
        o_ref[:, p * LANES:(p + 1) * LANES] = jnp.where(lane < V_HEAD, outs[2 * p], outs[2 * p + 1])


def _attention(q, k, v, B, S, tq):
    H = MLA_HEADS
    nq = S // tq
    return pl.pallas_call(
        functools.partial(_attn_kernel, tq=tq),
        grid=(B, nq),
        in_specs=[pl.BlockSpec((tq, H * HEAD_PAD), lambda b, i: (b * nq + i, 0)),
                  pl.BlockSpec((S, H * HEAD_PAD), lambda b, i: (b, 0)),
                  pl.BlockSpec((S, H * V_HEAD), lambda b, i: (b, 0))],
        out_specs=pl.BlockSpec((tq, H * V_HEAD), lambda b, i: (b * nq + i, 0)),
        out_shape=jax.ShapeDtypeStruct((B * S, H * V_HEAD), F32),
        compiler_params=_cp("parallel", "arbitrary"),
        name="mla_attention",
    )(q, k, v)


def _hgrn_kernel(hq_ref, hf_ref, hi_ref, hg_ref, lb_ref, ng_ref, o_ref, sout_ref,
                 st_ref, a_ref, b_ref, q_ref, k_ref):
    C = HG_CHUNK
    c = pl.program_id(1)

    @pl.when(c == 0)
    def _():
        st_ref[...] = jnp.zeros_like(st_ref)

    row = lax.broadcasted_iota(jnp.int32, (C, HG_DK), 0)
    row_c = lax.broadcasted_iota(jnp.int32, (C, C), 0)
    col_c = lax.broadcasted_iota(jnp.int32, (C, C), 1)
    sub8 = lax.broadcasted_iota(jnp.int32, (SUBLANES, C), 0)
    lane8 = lax.broadcasted_iota(jnp.int32, (SUBLANES, C), 1)
    ng = ng_ref[...]
    for h in range(HG_HEADS):
        sl = slice(h * HG_DK, (h + 1) * HG_DK)
        lbh = lb_ref[:, sl]
        f = lbh + (1.0 - lbh) * jax.nn.sigmoid(hf_ref[:, sl])
        kk = 1.0 - f
        q = _silu(hq_ref[:, sl]) * (HG_DK ** -0.5)
        v = hi_ref[:, sl]
        b = jnp.log(f)
        sh = 1
        while sh < C:
            b = b + jnp.where(row >= sh, pltpu.roll(b, sh, 0), 0.0)
            sh *= 2
        b_ref[...] = b
        q_ref[...] = q
        k_ref[...] = kk

        amat = jnp.zeros((C, C), F32)
        m = SUBLANES
        while 2 * m <= C:
            span = 2 * m
            bm = jnp.concatenate(
                [jnp.broadcast_to(b[st + m - 1:st + m, :], (span, HG_DK)) for st in range(0, C, span)], axis=0)
            e = jnp.exp(-jnp.abs(b - bm))
            second = (row & (span - 1)) >= m
            ql = jnp.where(second, q * e, 0.0).astype(BF16)
            kl = jnp.where(second, 0.0, kk * e).astype(BF16)
            al = _dot_nt(ql, kl)
            if span < C:
                shift = span.bit_length() - 1
                al = jnp.where((row_c >> shift) == (col_c >> shift), al, 0.0)
            amat = amat + al
            m *= 2

        def diag(i, carry):
            r0 = pl.multiple_of(i * SUBLANES, SUBLANES)
            bb = b_ref[pl.ds(r0, SUBLANES), :]
            qb = q_ref[pl.ds(r0, SUBLANES), :]
            kb = k_ref[pl.ds(r0, SUBLANES), :]
            arow = jnp.zeros((SUBLANES, C), F32)
            for j in range(SUBLANES):
                d = jnp.where(sub8[:, :HG_DK] >= j, bb - bb[j:j + 1, :], NEG)
                w = qb * kb[j:j + 1, :] * jnp.exp(d)
                colv = jnp.sum(w, axis=-1, keepdims=True)
                arow = jnp.where(lane8 == r0 + j, colv, arow)
            a_ref[pl.ds(r0, SUBLANES), :] = arow
            return carry

        lax.fori_loop(0, C // SUBLANES, diag, 0)
        amat = amat + a_ref[...]

        vb = v.astype(BF16)
        st = st_ref[h]
        o = _dot(amat.astype(BF16), vb) + _dot_nt((q * jnp.exp(b)).astype(BF16), st.astype(BF16))
        bl = b[C - 1:C, :]
        kdec = (kk * jnp.exp(bl - b)).astype(BF16)
        st_new = st * jnp.exp(bl) + _dot(v.T.astype(BF16), kdec)
        st_ref[h] = st_new
        o_ref[:, sl] = _rms(o, ng) * _silu(hg_ref[:, sl])

        @pl.when(c == pl.num_programs(1) - 1)
        def _(h=h, st_new=st_new):
            sout_ref[0, h] = st_new.T


def _hgrn(z, lb, ng, B, S):
    C = HG_CHUNK
    nc = S // C
    W = HG_WIDTH
    spec = lambda cb: pl.BlockSpec((C, W), lambda b, c, cb=cb: (b * nc + c, cb))
    return pl.pallas_call(
        _hgrn_kernel,
        grid=(B, nc),
        in_specs=[spec(0), spec(1), spec(2), spec(3),
                  pl.BlockSpec((1, W), lambda b, c: (0, 0)),
                  pl.BlockSpec((1, HG_DV), lambda b, c: (0, 0))],
        out_specs=[pl.BlockSpec((C, W), lambda b, c: (b * nc + c, 0)),
                   pl.BlockSpec((1, HG_HEADS, HG_DK, HG_DV), lambda b, c: (b, 0, 0, 0))],
        out_shape=[jax.ShapeDtypeStruct((B * S, W), F32),
                   jax.ShapeDtypeStruct((B, HG_HEADS, HG_DK, HG_DV), F32)],
        scratch_shapes=[pltpu.VMEM((HG_HEADS, HG_DV, HG_DK), F32),
                        pltpu.VMEM((C, C), F32),
                        pltpu.VMEM((C, HG_DK), F32), pltpu.VMEM((C, HG_DK), F32), pltpu.VMEM((C, HG_DK), F32)],
        compiler_params=_cp("parallel", "arbitrary"),
        name="hgrn_prompt",
    )(z, z, z, z, lb, ng)


def _layernorm(y, g, b):
    mu = jnp.mean(y, axis=-1, keepdims=True)
    d = y - mu
    var = jnp.mean(d * d, axis=-1, keepdims=True)
    return d * lax.rsqrt(var + EPS) * g + b


def _conv_kernel(glu_ref, w_ref, bias_ref, lg_ref, lb_ref, o_ref, st_ref, buf_ref, *, tc):
    c = pl.program_id(1)
    P = CONV_PAD
    first = P - (CONV_WIDTH - 1)
    rc = 32

    @pl.when(c == 0)
    def _():
        buf_ref[0:P, :] = jnp.zeros((P, CONV_CH), F32)

    g = glu_ref[...]
    buf_ref[P:P + tc, :] = g[:, :CONV_CH] * jax.nn.sigmoid(g[:, CONV_CH:])
    bias = bias_ref[...]
    lg = lg_ref[...]
    lb = lb_ref[...]
    for r in range(0, tc, rc):
        acc = jnp.broadcast_to(bias, (rc, CONV_CH))
        for j in range(CONV_WIDTH):
            acc = acc + w_ref[j:j + 1, :] * buf_ref[first + j + r:first + j + r + rc, :]
        o_ref[r:r + rc, :] = _silu(_layernorm(acc, lg, lb))

    @pl.when(c == pl.num_programs(1) - 1)
    def _():
        st_ref[0] = buf_ref[tc + first:tc + P, :]

    buf_ref[0:P, :] = buf_ref[tc:tc + P, :]


def _conv(z, w, bias, lg, lb, B, S, tc):
    nc = S // tc
    const = lambda b, c: (0, 0)
    gblk = HG_SEG // GLU_SEG
    return pl.pallas_call(
        functools.partial(_conv_kernel, tc=tc),
        grid=(B, nc),
        in_specs=[pl.BlockSpec((tc, GLU_SEG), lambda b, c: (b * nc + c, gblk)),
                  pl.BlockSpec((CONV_WIDTH, CONV_CH), const),
                  pl.BlockSpec((1, CONV_CH), const), pl.BlockSpec((1, CONV_CH), const),
                  pl.BlockSpec((1, CONV_CH), const)],
        out_specs=[pl.BlockSpec((tc, CONV_CH), lambda b, c: (b * nc + c, 0)),
                   pl.BlockSpec((1, CONV_WIDTH - 1, CONV_CH), lambda b, c: (b, 0, 0))],
        out_shape=[jax.ShapeDtypeStruct((B * S, CONV_CH), F32),
                   jax.ShapeDtypeStruct((B, CONV_WIDTH - 1, CONV_CH), F32)],
        scratch_shapes=[pltpu.VMEM((CONV_PAD + tc, CONV_CH), F32)],
        compiler_params=_cp("parallel", "arbitrary"),
        name="conv_prompt",
    )(z, w, bias, lg, lb)


def _post_kernel(x_ref, a_ref, h_ref, c_ref, gt_ref, wmo_ref, who_ref, wco_ref, wout_ref, o_ref):
    D = D_MODEL
    ya = _dot(a_ref[...].astype(BF16), wmo_ref[...])
    yh = _dot(h_ref[...].astype(BF16), who_ref[...])
    yc = _dot(c_ref[...].astype(BF16), wco_ref[...])
    merged = (jax.nn.sigmoid(gt_ref[:, 0:D]) * ya + jax.nn.sigmoid(gt_ref[:, D:2 * D]) * yh
              + jax.nn.sigmoid(gt_ref[:, 2 * D:3 * D]) * yc)
    o_ref[...] = x_ref[...] + _dot(merged.astype(BF16), wout_ref[...])


def _post(x, a, hg, cv, z, wmo, who, wco, wout, tm):
    T, D = x.shape
    gblk = (HG_SEG + GLU_SEG) // GATE_SEG
    row = lambda i: (i, 0)
    const = lambda i: (0, 0)
    return pl.pallas_call(
        _post_kernel,
        grid=(T // tm,),
        in_specs=[pl.BlockSpec((tm, D), row), pl.BlockSpec((tm, a.shape[1]), row),
                  pl.BlockSpec((tm, hg.shape[1]), row), pl.BlockSpec((tm, cv.shape[1]), row),
                  pl.BlockSpec((tm, GATE_SEG), lambda i: (i, gblk)),
                  pl.BlockSpec(wmo.shape, const), pl.BlockSpec(who.shape, const),
                  pl.BlockSpec(wco.shape, const), pl.BlockSpec(wout.shape, const)],
        out_specs=pl.BlockSpec((tm, D), row),
        out_shape=jax.ShapeDtypeStruct((T, D), F32),
        compiler_params=_cp("parallel"),
        name="merge_out",
    )(x, a, hg, cv, z, wmo, who, wco, wout)


def _mla_prep_s_kernel(z_ref, qn_ref, kvn_ref, wqn_ref, wqr_ref, wqs_ref, wukt_ref, cq_ref, sq_ref,
                       ck_ref, sk_ref, ql_ref, qp_ref, lat_ref, pe_ref):
    z = z_ref[...]
    cq = z[:, :Q_RANK]
    ckv = z[:, Q_RANK:Q_RANK + KV_RANK]
    o = Q_RANK + KV_RANK
    kpe = z[:, o:o + QK_ROPE]
    kpes = z[:, o + QK_ROPE:o + 2 * QK_ROPE]
    qn = _rms(cq, qn_ref[...]).astype(BF16)
    qnope = (_dot(qn, wqn_ref[...]) * MLA_SCALE).astype(BF16)
    for h in range(MLA_HEADS):
        ql_ref[:, h * KV_RANK:(h + 1) * KV_RANK] = _dot(qnope[:, h * HEAD_PAD:(h + 1) * HEAD_PAD], wukt_ref[h])
    qp_ref[...] = _dot(qn, wqr_ref[...]) * cq_ref[...] + _dot(qn, wqs_ref[...]) * sq_ref[...]
    lat_ref[...] = _rms(ckv, kvn_ref[...])
    pe_ref[...] = kpe * ck_ref[...] + kpes * sk_ref[...]


def _mla_prep_s(z, qn, kvn, wqn, wqr, wqs, wukt, cosq, sinq, cosk, sink):
    T = z.shape[0]
    H = MLA_HEADS
    zblk = (Z_WIDTH - MLA_SEG) // MLA_SEG
    full = lambda a: pl.BlockSpec(a.shape, lambda i, n=a.ndim: (0,) * n)
    return pl.pallas_call(
        _mla_prep_s_kernel,
        grid=(1,),
        in_specs=[pl.BlockSpec((T, MLA_SEG), lambda i: (0, zblk)),
                  full(qn), full(kvn), full(wqn), full(wqr), full(wqs), full(wukt),
                  full(cosq), full(sinq), full(cosk), full(sink)],
        out_specs=[pl.BlockSpec((T, H * KV_RANK), lambda i: (0, 0)),
                   pl.BlockSpec((T, H * QK_ROPE), lambda i: (0, 0)),
                   pl.BlockSpec((T, KV_RANK), lambda i: (0, 0)),
                   pl.BlockSpec((T, QK_ROPE), lambda i: (0, 0))],
        out_shape=[jax.ShapeDtypeStruct((T, H * KV_RANK), F32),
                   jax.ShapeDtypeStruct((T, H * QK_ROPE), F32),
                   jax.ShapeDtypeStruct((T, KV_RANK), F32),
                   jax.ShapeDtypeStruct((T, QK_ROPE), F32)],
        compiler_params=_cp("arbitrary"),
        name="mla_prep_sample",
    )(z, qn, kvn, wqn, wqr, wqs, wukt, cosq, sinq, cosk, sink)


def _decode_kernel(pt_ref, ql_ref, qp_ref, cn_ref, pn_ref, lat_hbm, pe_hbm, o_ref, latbuf, pebuf, sem,
                   *, layer, n_pages, pages_per_chunk):
    b = pl.program_id(0)
    nb = pl.num_programs(0)

    def copies(bb, slot, p):
        pg = pt_ref[bb, p]
        return (pltpu.make_async_copy(lat_hbm.at[layer, pg], latbuf.at[slot, p], sem.at[0, slot]),
                pltpu.make_async_copy(pe_hbm.at[layer, pg], pebuf.at[slot, p], sem.at[1, slot]))

    def issue(bb, slot):
        def body(p, carry):
            c_lat, c_pe = copies(bb, slot, p)
            c_lat.start()
            c_pe.start()
            return carry
        lax.fori_loop(0, n_pages, body, 0)

    @pl.when(b == 0)
    def _():
        issue(0, 0)

    @pl.when(b + 1 < nb)
    def _():
        issue(b + 1, (b + 1) % 2)

    slot = b % 2

    def wait_body(p, carry):
        c_lat, c_pe = copies(b, slot, p)
        c_lat.wait()
        c_pe.wait()
        return carry

    lax.fori_loop(0, n_pages, wait_body, 0)

    H = MLA_HEADS
    ql = ql_ref[0]
    qp = qp_ref[0]
    qlb = ql.astype(BF16)
    qpb = qp.astype(BF16)
    rows = pages_per_chunk * PAGE_SIZE

    def chunk(ci, carry):
        m, l, acc = carry
        p0 = ci * pages_per_chunk
        latc = latbuf[slot, pl.ds(p0, pages_per_chunk)].reshape(rows, KV_RANK).astype(BF16)
        pec = pebuf[slot, pl.ds(p0, pages_per_chunk)].reshape(rows, QK_ROPE).astype(BF16)
        s = _dot_nt(qlb, latc) + _dot_nt(qpb, pec)
        m_new = jnp.maximum(m, jnp.max(s, axis=-1, keepdims=True))
        a = jnp.exp(m - m_new)
        p = jnp.exp(s - m_new)
        l = a * l + jnp.sum(p, axis=-1, keepdims=True)
        acc = a * acc + _dot(p.astype(BF16), latc)
        return m_new, l, acc

    init = (jnp.full((H, 1), NEG, F32), jnp.zeros((H, 1), F32), jnp.zeros((H, KV_RANK), F32))
    m, l, acc = lax.fori_loop(0, n_pages // pages_per_chunk, chunk, init)
    cn = cn_ref[0]
    pn = pn_ref[0]
    s_new = jnp.sum(ql * cn, axis=-1, keepdims=True) + jnp.sum(qp * pn, axis=-1, keepdims=True)
    m_f = jnp.maximum(m, s_new)
    a = jnp.exp(m - m_f)
    p_new = jnp.exp(s_new - m_f)
    l = a * l + p_new
    o_ref[0] = (a * acc + p_new * cn) / l


def _decode_attention(page_table, ql, qp, lat_new, pe_new, cache_lat, cache_pe, layer):
    B, n_pages = page_table.shape
    H = MLA_HEADS
    ppc = 8
    grid_spec = pltpu.PrefetchScalarGridSpec(
        num_scalar_prefetch=1,
        grid=(B,),
        in_specs=[pl.BlockSpec((1, H, KV_RANK), lambda b, pt: (b, 0, 0)),
                  pl.BlockSpec((1, H, QK_ROPE), lambda b, pt: (b, 0, 0)),
                  pl.BlockSpec((1, 1, KV_RANK), lambda b, pt: (b, 0, 0)),
                  pl.BlockSpec((1, 1, QK_ROPE), lambda b, pt: (b, 0, 0)),
                  pl.BlockSpec(memory_space=pl.ANY),
                  pl.BlockSpec(memory_space=pl.ANY)],
        out_specs=pl.BlockSpec((1, H, KV_RANK), lambda b, pt: (b, 0, 0)),
        scratch_shapes=[pltpu.VMEM((2, n_pages, PAGE_SIZE, KV_RANK), F32),
                        pltpu.VMEM((2, n_pages, PAGE_SIZE, QK_ROPE), F32),
                        pltpu.SemaphoreType.DMA((2, 2))],
    )
    return pl.pallas_call(
        functools.partial(_decode_kernel, layer=layer, n_pages=n_pages, pages_per_chunk=ppc),
        grid_spec=grid_spec,
        out_shape=jax.ShapeDtypeStruct((B, H, KV_RANK), F32),
        compiler_params=_cp("arbitrary"),
        name="mla_decode",
    )(page_table, ql.reshape(B, H, KV_RANK), qp.reshape(B, H, QK_ROPE),
      lat_new.reshape(B, 1, KV_RANK), pe_new.reshape(B, 1, QK_ROPE), cache_lat, cache_pe)


def _uv_kernel(ol_ref, wv_ref, o_ref):
    acc = jnp.zeros(o_ref.shape, F32)
    for h in range(MLA_HEADS):
        acc = acc + _dot(ol_ref[:, h * KV_RANK:(h + 1) * KV_RANK].astype(BF16), wv_ref[h])
    o_ref[...] = acc


def _uv_proj(olat, wvp):
    T = olat.shape[0]
    return pl.pallas_call(
        _uv_kernel,
        out_shape=jax.ShapeDtypeStruct((T, MLA_HEADS * V_HEAD), F32),
        compiler_params=pltpu.CompilerParams(vmem_limit_bytes=VMEM_LIMIT),
        name="mla_uv_sample",
    )(olat, wvp)


def _hgrn_step_kernel(hq_ref, hf_ref, hi_ref, hg_ref, lb_ref, ng_ref, s0_ref, o_ref, s_ref, *, bt):
    ng = ng_ref[...]
    for bi in range(bt):
        for h in range(HG_HEADS):
            sl = slice(h * HG_DK, (h + 1) * HG_DK)
            r = slice(bi, bi + 1)
            lbh = lb_ref[:, sl]
            f = lbh + (1.0 - lbh) * jax.nn.sigmoid(hf_ref[r, sl])
            kk = 1.0 - f
            q = _silu(hq_ref[r, sl]) * (HG_DK ** -0.5)
            v = hi_ref[r, sl]
            fcol = jnp.broadcast_to(f, (HG_DK, HG_DK)).T
            kcol = jnp.broadcast_to(kk, (HG_DK, HG_DK)).T
            s_new = fcol * s0_ref[bi, h] + kcol * v
            s_ref[bi, h] = s_new
            o = _dot(jnp.broadcast_to(q, (SUBLANES, HG_DK)).astype(BF16), s_new.astype(BF16))[0:1]
            o_ref[r, sl] = _rms(o, ng) * _silu(hg_ref[r, sl])


def _hgrn_step(z, lb, ng, s0):
    B = z.shape[0]
    bt = SUBLANES
    W = HG_WIDTH
    spec = lambda cb: pl.BlockSpec((bt, W), lambda i, cb=cb: (i, cb))
    sspec = pl.BlockSpec((bt, HG_HEADS, HG_DK, HG_DV), lambda i: (i, 0, 0, 0))
    return pl.pallas_call(
        functools.partial(_hgrn_step_kernel, bt=bt),
        grid=(B // bt,),
        in_specs=[spec(0), spec(1), spec(2), spec(3),
                  pl.BlockSpec((1, W), lambda i: (0, 0)), pl.BlockSpec((1, HG_DV), lambda i: (0, 0)), sspec],
        out_specs=[pl.BlockSpec((bt, W), lambda i: (i, 0)), sspec],
        out_shape=[jax.ShapeDtypeStruct((B, W), F32), jax.ShapeDtypeStruct(s0.shape, F32)],
        compiler_params=_cp("parallel"),
        name="hgrn_sample",
    )(z, z, z, z, lb, ng, s0)


def _conv_step_kernel(glu_ref, st_ref, w_ref, bias_ref, lg_ref, lb_ref, o_ref, ns_ref, *, bt):
    K = CONV_WIDTH - 1
    g = glu_ref[...]
    u = g[:, :CONV_CH] * jax.nn.sigmoid(g[:, CONV_CH:])
    wp = w_ref[0:K, :]
    wl = w_ref[K:K + 1, :]
    for bi in range(bt):
        st = st_ref[bi]
        ub = u[bi:bi + 1]
        y = jnp.sum(st * wp, axis=0, keepdims=True) + wl * ub + bias_ref[...]
        o_ref[bi:bi + 1, :] = _silu(_layernorm(y, lg_ref[...], lb_ref[...]))
        ns_ref[bi, 0:K - 1, :] = st_ref[bi, 1:K, :]
        ns_ref[bi, K - 1:K, :] = ub


def _conv_step(z, st, w, bias, lg, lb):
    B = z.shape[0]
    bt = SUBLANES
    K = CONV_WIDTH - 1
    gblk = HG_SEG // GLU_SEG
    const = lambda i: (0, 0)
    sspec = pl.BlockSpec((bt, K, CONV_CH), lambda i: (i, 0, 0))
    return pl.pallas_call(
        functools.partial(_conv_step_kernel, bt=bt),
        grid=(B // bt,),
        in_specs=[pl.BlockSpec((bt, GLU_SEG), lambda i: (i, gblk)), sspec,
                  pl.BlockSpec((CONV_WIDTH, CONV_CH), const), pl.BlockSpec((1, CONV_CH), const),
                  pl.BlockSpec((1, CONV_CH), const), pl.BlockSpec((1, CONV_CH), const)],
        out_specs=[pl.BlockSpec((bt, CONV_CH), lambda i: (i, 0)), sspec],
        out_shape=[jax.ShapeDtypeStruct((B, CONV_CH), F32), jax.ShapeDtypeStruct(st.shape, F32)],
        compiler_params=_cp("parallel"),
        name="conv_sample",
    )(z, st, w, bias, lg, lb)


def _rope_tables(pos):
    half = QK_ROPE // 2
    inv = ROPE_THETA ** (-jnp.arange(half, dtype=F32) / half)
    ang = pos.astype(F32)[:, None] * inv[None, :]
    return jnp.cos(ang), jnp.sin(ang)


def _layer_params(l, w_in, mla_w_uq, mla_w_ukv):
    H = MLA_HEADS
    w = w_in[l]
    o_kv = Q_RANK
    o_pe = o_kv + KV_RANK
    o_hg = o_pe + QK_ROPE
    o_glu = o_hg + HG_SEG
    o_gate = o_glu + GLU_SEG
    kpe = w[:, o_pe:o_hg]
    half = QK_ROPE // 2
    w_z = jnp.concatenate(
        [w[:, o_hg:o_glu], w[:, o_glu:o_gate], w[:, o_gate:], w[:, :o_pe], kpe,
         kpe[:, half:], kpe[:, :half], jnp.zeros((D_MODEL, 64), F32)], axis=1).astype(BF16)

    uq = mla_w_uq[l].reshape(Q_RANK, H, QK_NOPE + QK_ROPE)
    nope, rope = uq[..., :QK_NOPE], uq[..., QK_NOPE:]
    rope_sw = jnp.concatenate([rope[..., half:], rope[..., :half]], axis=-1)
    z32 = jnp.zeros((Q_RANK, H, HEAD_PAD - QK_NOPE - QK_ROPE), F32)
    z64 = jnp.zeros((Q_RANK, H, QK_NOPE), F32)
    wqa = jnp.concatenate([nope, rope, z32], axis=-1).reshape(Q_RANK, H * HEAD_PAD).astype(BF16)
    wqb = jnp.concatenate([z64, rope_sw, z32], axis=-1).reshape(Q_RANK, H * HEAD_PAD).astype(BF16)
    wqn = jnp.concatenate([nope, z64], axis=-1).reshape(Q_RANK, H * HEAD_PAD).astype(BF16)
    wqr = rope.reshape(Q_RANK, H * QK_ROPE).astype(BF16)
    wqs = rope_sw.reshape(Q_RANK, H * QK_ROPE).astype(BF16)

    ukv = mla_w_ukv[l].reshape(KV_RANK, H, QK_NOPE + V_HEAD)
    uk, uv = ukv[..., :QK_NOPE], ukv[..., QK_NOPE:]
    wk = jnp.concatenate([uk, jnp.zeros((KV_RANK, H, HEAD_PAD - QK_NOPE), F32)], axis=-1)
    wk = wk.reshape(KV_RANK, H * HEAD_PAD).astype(BF16)
    wv = uv.reshape(KV_RANK, H * V_HEAD).astype(BF16)
    ukt = jnp.transpose(uk, (1, 2, 0))
    wukt = jnp.concatenate([ukt, jnp.zeros((H, HEAD_PAD - QK_NOPE, KV_RANK), F32)], axis=1).astype(BF16)
    head_of_col = jnp.arange(H * V_HEAD) // V_HEAD
    wvp = jnp.where(head_of_col[None, None, :] == jnp.arange(H)[:, None, None],
                    uv.reshape(KV_RANK, H * V_HEAD)[None], 0.0).astype(BF16)
    return dict(w_z=w_z, wqa=wqa, wqb=wqb, wqn=wqn, wqr=wqr, wqs=wqs, wk=wk, wv=wv, wukt=wukt, wvp=wvp)


def kernel(x_prompt, x_sample, cache_kv_latent, cache_k_rope, state_hgrn, state_conv, page_table,
           norm_ffa, w_ffa_gate, w_ffa_up, w_ffa_down, norm_mix, w_in,
           mla_q_norm, mla_w_uq, mla_kv_norm, mla_w_ukv, mla_w_o,
           hg_lower_bounds, hg_norm, hg_w_o,
           conv_w, conv_b, conv_ln_g, conv_ln_b, conv_w_o, w_out,
           norm_ffb, w_ffb_gate, w_ffb_up, w_ffb_down, norm_final):
    B, S, D = x_prompt.shape
    Bs, Ts, _ = x_sample.shape
    assert Ts == 1, "the sample kernels advance exactly one token per sequence"
    past_len = page_table.shape[1] * PAGE_SIZE
    H = MLA_HEADS
    TM = 1024 if (B * S) % 1024 == 0 else S
    TM_MIX = 512 if S % 512 == 0 else S
    TQ = 256
    TC = 256

    lb_all = _lower_bounds(hg_lower_bounds)

    half = QK_ROPE // 2
    cos_p, sin_p = _rope_tables(jnp.arange(S))
    cos_s, sin_s = _rope_tables(past_len + jnp.arange(Ts))
    ones = jnp.ones((S, QK_NOPE), F32)
    zpad = jnp.zeros((S, HEAD_PAD - QK_NOPE - QK_ROPE), F32)
    cosq_p = jnp.concatenate([ones, cos_p, cos_p, zpad], axis=1) * MLA_SCALE
    sinq_p = jnp.concatenate([0.0 * ones, -sin_p, sin_p, zpad], axis=1) * MLA_SCALE
    cosk_p = jnp.concatenate([cos_p, cos_p], axis=1)
    sink_p = jnp.concatenate([-sin_p, sin_p], axis=1)
    cosk_s = jnp.concatenate([cos_s, cos_s], axis=1)
    sink_s = jnp.concatenate([-sin_s, sin_s], axis=1)
    cosq_s = jnp.tile(cosk_s, (1, H)) * MLA_SCALE
    sinq_s = jnp.tile(sink_s, (1, H)) * MLA_SCALE
    pk = (jnp.arange(H * HEAD_PAD)[None, :] % HEAD_PAD == QK_NOPE + jnp.arange(QK_ROPE)[:, None]).astype(BF16)

    row = lambda a, l: a[l][None, :]
    bf = lambda a, l: a[l].astype(BF16)

    xp = x_prompt.reshape(B * S, D)
    xs = x_sample.reshape(Bs * Ts, D)
    outs = [[] for _ in range(8)]
    for l in range(DEPTH):
        lp = _layer_params(l, w_in, mla_w_uq, mla_w_ukv)
        ffa = (row(norm_ffa, l), bf(w_ffa_gate, l), bf(w_ffa_up, l), bf(w_ffa_down, l))
        ffb = (row(norm_ffb, l), bf(w_ffb_gate, l), bf(w_ffb_up, l), bf(w_ffb_down, l))
        wo = (bf(mla_w_o, l), bf(hg_w_o, l), bf(conv_w_o, l), bf(w_out, l))
        lb = lb_all[l][None, :]
        ng = row(hg_norm, l)
        cvp = (conv_w[l], row(conv_b, l), row(conv_ln_g, l), row(conv_ln_b, l))

        xp = _ffn(xp, *ffa, tm=TM)
        z = _inproj(xp, row(norm_mix, l), lp["w_z"], tm=TM)
        q, lat, pe, k, v = _mla_prep(z, S, row(mla_q_norm, l), row(mla_kv_norm, l), lp["wqa"], lp["wqb"],
                                     cosq_p, sinq_p, lp["wk"], lp["wv"], pk, cosk_p, sink_p, tm=TM_MIX)
        att = _attention(q, k, v, B, S, TQ)
        hg, hg_state = _hgrn(z, lb, ng, B, S)
        cv, cv_state = _conv(z, *cvp, B=B, S=S, tc=TC)
        xp = _post(xp, att, hg, cv, z, *wo, tm=TM_MIX)
        xp = _ffn(xp, *ffb, tm=TM)
        outs[0].append(lat.reshape(B, S, KV_RANK))
        outs[1].append(pe.reshape(B, S, QK_ROPE))
        outs[2].append(hg_state)
        outs[3].append(cv_state)

        xs = _ffn(xs, *ffa, tm=Bs)
        zs = _inproj(xs, row(norm_mix, l), lp["w_z"], tm=Bs)
        ql, qp, lat_s, pe_s = _mla_prep_s(zs, row(mla_q_norm, l), row(mla_kv_norm, l), lp["wqn"], lp["wqr"],
                                          lp["wqs"], lp["wukt"], cosq_s, sinq_s, cosk_s, sink_s)
        olat = _decode_attention(page_table, ql, qp, lat_s, pe_s, cache_kv_latent, cache_k_rope, l)
        att_s = _uv_proj(olat.reshape(Bs, H * KV_RANK), lp["wvp"])
        hg_s, hg_state_s = _hgrn_step(zs, lb, ng, state_hgrn[l])
        cv_s, cv_state_s = _conv_step(zs, state_conv[l], *cvp)
        xs = _post(xs, att_s, hg_s, cv_s, zs, *wo, tm=Bs)
        xs = _ffn(xs, *ffb, tm=Bs)
        outs[4].append(lat_s.reshape(Bs, Ts, KV_RANK))
        outs[5].append(pe_s.reshape(Bs, Ts, QK_ROPE))
        outs[6].append(hg_state_s)
        outs[7].append(cv_state_s)

    y_prompt = _final_norm(xp, norm_final[None, :], TM).reshape(B, S, D)
    y_sample = _final_norm(xs, norm_final[None, :], Bs).reshape(Bs, Ts, D)
    return (y_prompt, y_sample) + tuple(jnp.stack(o) for o in outs)
```

```python
import functools
import math

import jax
import jax.numpy as jnp
from jax import lax
from jax.experimental import pallas as pl
from jax.experimental.pallas import tpu as pltpu

F32 = jnp.float32
BF16 = jnp.bfloat16

D_MODEL = 1024
DEPTH = 4
PAGE_SIZE = 128
MLA_HEADS = 8
QK_NOPE = 64
QK_ROPE = 32
V_HEAD = 64
Q_RANK = 384
KV_RANK = 256
ROPE_THETA = 10000.0
MLA_SCALE = 1.0 / math.sqrt(QK_NOPE + QK_ROPE)
HG_HEADS = 4
HG_DK = 128
HG_DV = 128
HG_WIDTH = HG_HEADS * HG_DK
CONV_CH = 512
CONV_WIDTH = 31
EPS = 1e-6

LANES = 128
SUBLANES = 8
HEAD_PAD = 128
MLA_SEG = Q_RANK + KV_RANK + 2 * QK_ROPE + 64
HG_SEG = 4 * HG_WIDTH
GLU_SEG = 2 * CONV_CH
GATE_SEG = 3 * D_MODEL
Z_WIDTH = HG_SEG + GLU_SEG + GATE_SEG + MLA_SEG
VMEM_LIMIT = 56 * 1024 * 1024
NEG = -1e30
LOG2E = 1.4426950408889634
ATTN_HEAD_GROUP = 4
HG_CHUNK = 128
CONV_PAD = 32


def _cp(*sem):
    return pltpu.CompilerParams(dimension_semantics=sem, vmem_limit_bytes=VMEM_LIMIT)


def _rms(x, g):
    ms = jnp.mean(x * x, axis=-1, keepdims=True)
    return x * lax.rsqrt(ms + EPS) * g


def _silu(x):
    return x * jax.nn.sigmoid(x)


def _dot(a, b):
    return jnp.dot(a, b, preferred_element_type=F32)


def _dot_nt(a, b):
    return lax.dot_general(a, b, (((1,), (1,)), ((), ())), preferred_element_type=F32)


def _ffn_kernel(x_ref, g_ref, wg_ref, wu_ref, wd_ref, o_ref, h_ref, acc_ref):
    j = pl.program_id(1)

    @pl.when(j == 0)
    def _():
        h_ref[...] = _rms(x_ref[...], g_ref[...]).astype(BF16)
        acc_ref[...] = jnp.zeros_like(acc_ref)

    h = h_ref[...]
    g = _dot(h, wg_ref[...])
    u = _dot(h, wu_ref[...])
    a = (_silu(g) * u).astype(BF16)
    acc_ref[...] += _dot(a, wd_ref[...])

    @pl.when(j == pl.num_programs(1) - 1)
    def _():
        o_ref[...] = x_ref[...] + 0.5 * acc_ref[...]


def _ffn(x, g, wg, wu, wd, tm):
    T, D = x.shape
    FF = wg.shape[1]
    tf = 256
    return pl.pallas_call(
        _ffn_kernel,
        grid=(T // tm, FF // tf),
        in_specs=[pl.BlockSpec((tm, D), lambda i, j: (i, 0)),
                  pl.BlockSpec((1, D), lambda i, j: (0, 0)),
                  pl.BlockSpec((D, tf), lambda i, j: (0, j)),
                  pl.BlockSpec((D, tf), lambda i, j: (0, j)),
                  pl.BlockSpec((tf, D), lambda i, j: (j, 0))],
        out_specs=pl.BlockSpec((tm, D), lambda i, j: (i, 0)),
        out_shape=jax.ShapeDtypeStruct((T, D), F32),
        scratch_shapes=[pltpu.VMEM((tm, D), BF16), pltpu.VMEM((tm, D), F32)],
        compiler_params=_cp("parallel", "arbitrary"),
        name="ffn",
    )(x, g, wg, wu, wd)


def _inproj_kernel(x_ref, g_ref, w_ref, o_ref, h_ref):
    @pl.when(pl.program_id(1) == 0)
    def _():
        h_ref[...] = _rms(x_ref[...], g_ref[...]).astype(BF16)

    o_ref[...] = _dot(h_ref[...], w_ref[...])


def _inproj(x, g, w, tm):
    T, D = x.shape
    N = w.shape[1]
    tn = 1152
    return pl.pallas_call(
        _inproj_kernel,
        grid=(T // tm, N // tn),
        in_specs=[pl.BlockSpec((tm, D), lambda i, j: (i, 0)),
                  pl.BlockSpec((1, D), lambda i, j: (0, 0)),
                  pl.BlockSpec((D, tn), lambda i, j: (0, j))],
        out_specs=pl.BlockSpec((tm, tn), lambda i, j: (i, j)),
        out_shape=jax.ShapeDtypeStruct((T, N), F32),
        scratch_shapes=[pltpu.VMEM((tm, D), BF16)],
        compiler_params=_cp("parallel", "arbitrary"),
        name="inproj",
    )(x, g, w)


def _norm_kernel(x_ref, g_ref, o_ref):
    o_ref[...] = _rms(x_ref[...], g_ref[...])


def _final_norm(x, g, tm):
    T, D = x.shape
    return pl.pallas_call(
        _norm_kernel,
        grid=(T // tm,),
        in_specs=[pl.BlockSpec((tm, D), lambda i: (i, 0)), pl.BlockSpec((1, D), lambda i: (0, 0))],
        out_specs=pl.BlockSpec((tm, D), lambda i: (i, 0)),
        out_shape=jax.ShapeDtypeStruct((T, D), F32),
        compiler_params=_cp("parallel"),
        name="final_norm",
    )(x, g)


def _lb_kernel(x_ref, o_ref):
    x = x_ref[...]
    e = jnp.exp(x - jnp.max(x, axis=0, keepdims=True))
    p = e / jnp.sum(e, axis=0, keepdims=True)
    rows = [p[0:1]]
    for l in range(1, x.shape[0]):
        rows.append(rows[-1] + p[l:l + 1])
    c = jnp.concatenate(rows, axis=0)
    o_ref[...] = c - c[0:1]


def _lower_bounds(x):
    return pl.pallas_call(_lb_kernel, out_shape=jax.ShapeDtypeStruct(x.shape, F32), name="hg_lower_bounds")(x)


def _mla_prep_kernel(z_ref, qn_ref, kvn_ref, wqa_ref, wqb_ref, cq_ref, sq_ref, wk_ref, wv_ref, pk_ref,
                     ck_ref, sk_ref, q_ref, lat_ref, pe_ref, k_ref, v_ref):
    z = z_ref[...]
    cq = z[:, :Q_RANK]
    ckv = z[:, Q_RANK:Q_RANK + KV_RANK]
    o = Q_RANK + KV_RANK
    kpe = z[:, o:o + QK_ROPE]
    kpes = z[:, o + QK_ROPE:o + 2 * QK_ROPE]
    qn = _rms(cq, qn_ref[...]).astype(BF16)
    qa = _dot(qn, wqa_ref[...])
    qb = _dot(qn, wqb_ref[...])
    cs = cq_ref[...]
    sn = sq_ref[...]
    for h in range(MLA_HEADS):
        sl = slice(h * HEAD_PAD, (h + 1) * HEAD_PAD)
        q_ref[:, sl] = (qa[:, sl] * cs + qb[:, sl] * sn).astype(BF16)
    lat = _rms(ckv, kvn_ref[...])
    lat_ref[...] = lat
    per = kpe * ck_ref[...] + kpes * sk_ref[...]
    pe_ref[...] = per
    latb = lat.astype(BF16)
    k_ref[...] = (_dot(latb, wk_ref[...]) + _dot(per.astype(BF16), pk_ref[...])).astype(BF16)
    vt = _dot_nt(wv_ref[...], latb).astype(BF16)
    tk = v_ref.shape[2]
    for c in range(v_ref.shape[0]):
        v_ref[c] = vt[:, c * tk:(c + 1) * tk]


def _mla_prep(z, S, qn, kvn, wqa, wqb, cosq, sinq, wk, wv, pk, cosk, sink, tm, tk):
    T = z.shape[0]
    nb = S // tm
    zblk = (Z_WIDTH - MLA_SEG) // MLA_SEG
    const = lambda i: (0, 0)
    pos = lambda i: (i % nb, 0)
    H = MLA_HEADS
    return pl.pallas_call(
        _mla_prep_kernel,
        grid=(T // tm,),
        in_specs=[pl.BlockSpec((tm, MLA_SEG), lambda i: (i, zblk)),
                  pl.BlockSpec((1, Q_RANK), const), pl.BlockSpec((1, KV_RANK), const),
                  pl.BlockSpec((Q_RANK, H * HEAD_PAD), const), pl.BlockSpec((Q_RANK, H * HEAD_PAD), const),
                  pl.BlockSpec((tm, HEAD_PAD), pos), pl.BlockSpec((tm, HEAD_PAD), pos),
                  pl.BlockSpec((KV_RANK, H * HEAD_PAD), const), pl.BlockSpec((H * V_HEAD, KV_RANK), const),
                  pl.BlockSpec((QK_ROPE, H * HEAD_PAD), const),
                  pl.BlockSpec((tm, QK_ROPE), pos), pl.BlockSpec((tm, QK_ROPE), pos)],
        out_specs=[pl.BlockSpec((tm, H * HEAD_PAD), lambda i: (i, 0)),
                   pl.BlockSpec((tm, KV_RANK), lambda i: (i, 0)),
                   pl.BlockSpec((tm, QK_ROPE), lambda i: (i, 0)),
                   pl.BlockSpec((tm, H * HEAD_PAD), lambda i: (i, 0)),
                   pl.BlockSpec((tm // tk, H * V_HEAD, tk), lambda i: (i, 0, 0))],
        out_shape=[jax.ShapeDtypeStruct((T, H * HEAD_PAD), BF16),
                   jax.ShapeDtypeStruct((T, KV_RANK), F32),
                   jax.ShapeDtypeStruct((T, QK_ROPE), F32),
                   jax.ShapeDtypeStruct((T, H * HEAD_PAD), BF16),
                   jax.ShapeDtypeStruct((T // tk, H * V_HEAD, tk), BF16)],
        compiler_params=_cp("parallel"),
        name="mla_prep",
    )(z, qn, kvn, wqa, wqb, cosq, sinq, wk, wv, pk, cosk, sink)


def _attn_kernel(q_ref, k_ref, v_ref, o_ref, *, tq):
    qi = pl.program_id(1)
    krow = lax.broadcasted_iota(jnp.int32, (tq, tq), 0)
    qcol = lax.broadcasted_iota(jnp.int32, (tq, tq), 1)
    causal = krow <= qcol
    outs = []
    for g0 in range(0, MLA_HEADS, ATTN_HEAD_GROUP):
        heads = tuple(range(g0, g0 + ATTN_HEAD_GROUP))
        qh = [q_ref[:, h * HEAD_PAD:(h + 1) * HEAD_PAD] for h in heads]

        def step(j, carry, masked, heads=heads, qh=qh):
            off = pl.multiple_of(j * tq, tq)
            new = []
            for idx, h in enumerate(heads):
                m, l, acc = carry[idx]
                k = k_ref[pl.ds(off, tq), h * HEAD_PAD:(h + 1) * HEAD_PAD]
                vt = v_ref[j, h * V_HEAD:(h + 1) * V_HEAD, :]
                st = _dot_nt(k, qh[idx])
                if masked:
                    st = jnp.where(causal, st, NEG)
                m_new = jnp.maximum(m, jnp.max(st, axis=0, keepdims=True))
                a = jnp.exp2(m - m_new)
                p = jnp.exp2(st - m_new)
                l = a * l + jnp.sum(p, axis=0, keepdims=True)
                acc = a * acc + _dot(vt, p.astype(BF16))
                new.append((m_new, l, acc))
            return tuple(new)

        init = tuple((jnp.full((1, tq), NEG, F32), jnp.zeros((1, tq), F32), jnp.zeros((V_HEAD, tq), F32))
                     for _ in heads)
        carry = lax.fori_loop(0, qi, lambda j, c, step=step: step(j, c, False), init)
        carry = step(qi, carry, True)
        outs.extend(acc / l for _, l, acc in carry)
    for p in range(MLA_HEADS // 2):
        pair = jnp.concatenate([outs[2 * p], outs[2 * p + 1]], axis=0)
        o_ref[:, p * LANES:(p + 1) * LANES] = pair.T


def _attention(q, k, v, B, S, tq):
    H = MLA_HEADS
    nq = S // tq
    return pl.pallas_call(
        functools.partial(_attn_kernel, tq=tq),
        grid=(B, nq),
        in_specs=[pl.BlockSpec((tq, H * HEAD_PAD), lambda b, i: (b * nq + i, 0)),
                  pl.BlockSpec((S, H * HEAD_PAD), lambda b, i: (b, 0)),
                  pl.BlockSpec((nq, H * V_HEAD, tq), lambda b, i: (b, 0, 0))],
        out_specs=pl.BlockSpec((tq, H * V_HEAD), lambda b, i: (b * nq + i, 0)),
        out_shape=jax.ShapeDtypeStruct((B * S, H * V_HEAD), F32),
        compiler_params=_cp("parallel", "arbitrary"),
        name="mla_attention",
    )(q, k, v)


def _hgrn_kernel(hq_ref, hf_ref, hi_ref, hg_ref, lb_ref, ng_ref, o_ref, sout_ref,
                 st_ref, a_ref, b_ref, q_ref, k_ref):
    C = HG_CHUNK
    c = pl.program_id(1)

    @pl.when(c == 0)
    def _():
        st_ref[...] = jnp.zeros_like(st_ref)

    row = lax.broadcasted_iota(jnp.int32, (C, HG_DK), 0)
    row_c = lax.broadcasted_iota(jnp.int32, (C, C), 0)
    col_c = lax.broadcasted_iota(jnp.int32, (C, C), 1)
    sub8 = lax.broadcasted_iota(jnp.int32, (SUBLANES, HG_DK), 0)
    lane8 = lax.broadcasted_iota(jnp.int32, (SUBLANES, C), 1)
    ng = ng_ref[...]
    for h in range(HG_HEADS):
        sl = slice(h * HG_DK, (h + 1) * HG_DK)
        lbh = lb_ref[:, sl]
        f = lbh + (1.0 - lbh) * jax.nn.sigmoid(hf_ref[:, sl])
        kk = 1.0 - f
        q = _silu(hq_ref[:, sl]) * (HG_DK ** -0.5)
        b = jnp.log(f)
        sh = 1
        while sh < C:
            b = b + jnp.where(row >= sh, pltpu.roll(b, sh, 0), 0.0)
            sh *= 2
        b_ref[h] = b
        q_ref[h] = q
        k_ref[h] = kk

        amat = jnp.zeros((C, C), F32)
        m = SUBLANES
        while 2 * m <= C:
            span = 2 * m
            bm = jnp.concatenate(
                [jnp.broadcast_to(b[st + m - 1:st + m, :], (span, HG_DK)) for st in range(0, C, span)], axis=0)
            e = jnp.exp(-jnp.abs(b - bm))
            second = (row & (span - 1)) >= m
            ql = jnp.where(second, q * e, 0.0).astype(BF16)
            kl = jnp.where(second, 0.0, kk * e).astype(BF16)
            al = _dot_nt(ql, kl)
            if span < C:
                shift = span.bit_length() - 1
                al = jnp.where((row_c >> shift) == (col_c >> shift), al, 0.0)
            amat = amat + al
            m *= 2
        a_ref[h] = amat

    def diag(i, carry):
        r0 = pl.multiple_of(i * SUBLANES, SUBLANES)
        for h in range(HG_HEADS):
            bb = b_ref[h, pl.ds(r0, SUBLANES), :]
            qb = q_ref[h, pl.ds(r0, SUBLANES), :]
            kb = k_ref[h, pl.ds(r0, SUBLANES), :]
            arow = a_ref[h, pl.ds(r0, SUBLANES), :]
            for j in range(SUBLANES):
                d = jnp.where(sub8 >= j, bb - bb[j:j + 1, :], NEG)
                w = qb * kb[j:j + 1, :] * jnp.exp(d)
                colv = jnp.sum(w, axis=-1, keepdims=True)
                arow = jnp.where(lane8 == r0 + j, colv, arow)
            a_ref[h, pl.ds(r0, SUBLANES), :] = arow
        return carry

    lax.fori_loop(0, C // SUBLANES, diag, 0, unroll=2)

    for h in range(HG_HEADS):
        sl = slice(h * HG_DK, (h + 1) * HG_DK)
        b = b_ref[h]
        q = q_ref[h]
        kk = k_ref[h]
        v = hi_ref[:, sl]
        amat = a_ref[h]
        vb = v.astype(BF16)
        st = st_ref[h]
        o = _dot(amat.astype(BF16), vb) + _dot_nt((q * jnp.exp(b)).astype(BF16), st.astype(BF16))
        bl = b[C - 1:C, :]
        kdec = (kk * jnp.exp(bl - b)).astype(BF16)
        st_new = st * jnp.exp(bl) + _dot(v.T.astype(BF16), kdec)
        st_ref[h] = st_new
        o_ref[:, sl] = _rms(o, ng) * _silu(hg_ref[:, sl])

        @pl.when(c == pl.num_programs(1) - 1)
        def _(h=h, st_new=st_new):
            sout_ref[0, h] = st_new.T


def _hgrn(z, lb, ng, B, S):
    C = HG_CHUNK
    nc = S // C
    W = HG_WIDTH
    spec = lambda cb: pl.BlockSpec((C, W), lambda b, c, cb=cb: (b * nc + c, cb))
    return pl.pallas_call(
        _hgrn_kernel,
        grid=(B, nc),
        in_specs=[spec(0), spec(1), spec(2), spec(3),
                  pl.BlockSpec((1, W), lambda b, c: (0, 0)),
                  pl.BlockSpec((1, HG_DV), lambda b, c: (0, 0))],
        out_specs=[pl.BlockSpec((C, W), lambda b, c: (b * nc + c, 0)),
                   pl.BlockSpec((1, HG_HEADS, HG_DK, HG_DV), lambda b, c: (b, 0, 0, 0))],
        out_shape=[jax.ShapeDtypeStruct((B * S, W), F32),
                   jax.ShapeDtypeStruct((B, HG_HEADS, HG_DK, HG_DV), F32)],
        scratch_shapes=[pltpu.VMEM((HG_HEADS, HG_DV, HG_DK), F32),
                        pltpu.VMEM((HG_HEADS, C, C), F32),
                        pltpu.VMEM((HG_HEADS, C, HG_DK), F32), pltpu.VMEM((HG_HEADS, C, HG_DK), F32),
                        pltpu.VMEM((HG_HEADS, C, HG_DK), F32)],
        compiler_params=_cp("parallel", "arbitrary"),
        name="hgrn_prompt",
    )(z, z, z, z, lb, ng)


def _layernorm(y, g, b):
    mu = jnp.mean(y, axis=-1, keepdims=True)
    d = y - mu
    var = jnp.mean(d * d, axis=-1, keepdims=True)
    return d * lax.rsqrt(var + EPS) * g + b


def _conv_kernel(glu_ref, w_ref, bias_ref, lg_ref, lb_ref, o_ref, st_ref, buf_ref, sh_ref, *, tc):
    c = pl.program_id(1)
    P = CONV_PAD
    first = P - (CONV_WIDTH - 1)
    rc = 32
    span = tc + P - SUBLANES

    @pl.when(c == 0)
    def _():
        buf_ref[0:P, :] = jnp.zeros((P, CONV_CH), F32)

    g = glu_ref[...]
    buf_ref[P:P + tc, :] = g[:, :CONV_CH] * jax.nn.sigmoid(g[:, CONV_CH:])
    for s in range(1, SUBLANES):
        sh_ref[s - 1] = buf_ref[s:s + span, :]
    bias = bias_ref[...]
    lg = lg_ref[...]
    lb = lb_ref[...]
    for r in range(0, tc, rc):
        acc = jnp.broadcast_to(bias, (rc, CONV_CH))
        for j in range(CONV_WIDTH):
            base, s = divmod(first + j, SUBLANES)
            lo = base * SUBLANES + r
            win = buf_ref[lo:lo + rc, :] if s == 0 else sh_ref[s - 1, lo:lo + rc, :]
            acc = acc + w_ref[j:j + 1, :] * win
        o_ref[r:r + rc, :] = _silu(_layernorm(acc, lg, lb))

    @pl.when(c == pl.num_programs(1) - 1)
    def _():
        st_ref[0] = buf_ref[tc + first:tc + P, :]

    buf_ref[0:P, :] = buf_ref[tc:tc + P, :]


def _conv(z, w, bias, lg, lb, B, S, tc):
    nc = S // tc
    const = lambda b, c: (0, 0)
    gblk = HG_SEG // GLU_SEG
    return pl.pallas_call(
        functools.partial(_conv_kernel, tc=tc),
        grid=(B, nc),
        in_specs=[pl.BlockSpec((tc, GLU_SEG), lambda b, c: (b * nc + c, gblk)),
                  pl.BlockSpec((CONV_WIDTH, CONV_CH), const),
                  pl.BlockSpec((1, CONV_CH), const), pl.BlockSpec((1, CONV_CH), const),
                  pl.BlockSpec((1, CONV_CH), const)],
        out_specs=[pl.BlockSpec((tc, CONV_CH), lambda b, c: (b * nc + c, 0)),
                   pl.BlockSpec((1, CONV_WIDTH - 1, CONV_CH), lambda b, c: (b, 0, 0))],
        out_shape=[jax.ShapeDtypeStruct((B * S, CONV_CH), F32),
                   jax.ShapeDtypeStruct((B, CONV_WIDTH - 1, CONV_CH), F32)],
        scratch_shapes=[pltpu.VMEM((CONV_PAD + tc, CONV_CH), F32),
                        pltpu.VMEM((SUBLANES - 1, tc + CONV_PAD - SUBLANES, CONV_CH), F32)],
        compiler_params=_cp("parallel", "arbitrary"),
        name="conv_prompt",
    )(z, w, bias, lg, lb)


def _post_kernel(x_ref, a_ref, h_ref, c_ref, gt_ref, wmo_ref, who_ref, wco_ref, wout_ref, o_ref):
    D = D_MODEL
    ya = _dot(a_ref[...].astype(BF16), wmo_ref[...])
    yh = _dot(h_ref[...].astype(BF16), who_ref[...])
    yc = _dot(c_ref[...].astype(BF16), wco_ref[...])
    merged = (jax.nn.sigmoid(gt_ref[:, 0:D]) * ya + jax.nn.sigmoid(gt_ref[:, D:2 * D]) * yh
              + jax.nn.sigmoid(gt_ref[:, 2 * D:3 * D]) * yc)
    o_ref[...] = x_ref[...] + _dot(merged.astype(BF16), wout_ref[...])


def _post(x, a, hg, cv, z, wmo, who, wco, wout, tm):
    T, D = x.shape
    gblk = (HG_SEG + GLU_SEG) // GATE_SEG
    row = lambda i: (i, 0)
    const = lambda i: (0, 0)
    return pl.pallas_call(
        _post_kernel,
        grid=(T // tm,),
        in_specs=[pl.BlockSpec((tm, D), row), pl.BlockSpec((tm, a.shape[1]), row),
                  pl.BlockSpec((tm, hg.shape[1]), row), pl.BlockSpec((tm, cv.shape[1]), row),
                  pl.BlockSpec((tm, GATE_SEG), lambda i: (i, gblk)),
                  pl.BlockSpec(wmo.shape, const), pl.BlockSpec(who.shape, const),
                  pl.BlockSpec(wco.shape, const), pl.BlockSpec(wout.shape, const)],
        out_specs=pl.BlockSpec((tm, D), row),
        out_shape=jax.ShapeDtypeStruct((T, D), F32),
        compiler_params=_cp("parallel"),
        name="merge_out",
    )(x, a, hg, cv, z, wmo, who, wco, wout)


def _mla_prep_s_kernel(z_ref, qn_ref, kvn_ref, wqn_ref, wqr_ref, wqs_ref, wukt_ref, cq_ref, sq_ref,
                       ck_ref, sk_ref, ql_ref, qp_ref, lat_ref, pe_ref):
    z = z_ref[...]
    cq = z[:, :Q_RANK]
    ckv = z[:, Q_RANK:Q_RANK + KV_RANK]
    o = Q_RANK + KV_RANK
    kpe = z[:, o:o + QK_ROPE]
    kpes = z[:, o + QK_ROPE:o + 2 * QK_ROPE]
    qn = _rms(cq, qn_ref[...]).astype(BF16)
    qnope = (_dot(qn, wqn_ref[...]) * MLA_SCALE).astype(BF16)
    for h in range(MLA_HEADS):
        ql_ref[:, h * KV_RANK:(h + 1) * KV_RANK] = _dot(qnope[:, h * HEAD_PAD:(h + 1) * HEAD_PAD], wukt_ref[h])
    qp_ref[...] = _dot(qn, wqr_ref[...]) * cq_ref[...] + _dot(qn, wqs_ref[...]) * sq_ref[...]
    lat_ref[...] = _rms(ckv, kvn_ref[...])
    pe_ref[...] = kpe * ck_ref[...] + kpes * sk_ref[...]


def _mla_prep_s(z, qn, kvn, wqn, wqr, wqs, wukt, cosq, sinq, cosk, sink):
    T = z.shape[0]
    H = MLA_HEADS
    zblk = (Z_WIDTH - MLA_SEG) // MLA_SEG
    full = lambda a: pl.BlockSpec(a.shape, lambda i, n=a.ndim: (0,) * n)
    return pl.pallas_call(
        _mla_prep_s_kernel,
        grid=(1,),
        in_specs=[pl.BlockSpec((T, MLA_SEG), lambda i: (0, zblk)),
                  full(qn), full(kvn), full(wqn), full(wqr), full(wqs), full(wukt),
                  full(cosq), full(sinq), full(cosk), full(sink)],
        out_specs=[pl.BlockSpec((T, H * KV_RANK), lambda i: (0, 0)),
                   pl.BlockSpec((T, H * QK_ROPE), lambda i: (0, 0)),
                   pl.BlockSpec((T, KV_RANK), lambda i: (0, 0)),
                   pl.BlockSpec((T, QK_ROPE), lambda i: (0, 0))],
        out_shape=[jax.ShapeDtypeStruct((T, H * KV_RANK), F32),
                   jax.ShapeDtypeStruct((T, H * QK_ROPE), F32),
                   jax.ShapeDtypeStruct((T, KV_RANK), F32),
                   jax.ShapeDtypeStruct((T, QK_ROPE), F32)],
        compiler_params=_cp("arbitrary"),
        name="mla_prep_sample",
    )(z, qn, kvn, wqn, wqr, wqs, wukt, cosq, sinq, cosk, sink)


def _decode_kernel(pt_ref, ql_ref, qp_ref, cn_ref, pn_ref, lat_hbm, pet_hbm, o_ref,
                   latbuf, pebuf, latbf, s_ref, sem, *, layer, n_pages, pages_per_chunk):
    b = pl.program_id(0)
    nb = pl.num_programs(0)

    def issue(bb, slot):
        def body(p, carry):
            pg = pt_ref[bb, p]
            pltpu.make_async_copy(lat_hbm.at[layer, pg], latbuf.at[slot, p], sem.at[0, slot]).start()
            pltpu.make_async_copy(pet_hbm.at[layer, pg], pebuf.at[slot, p], sem.at[1, slot]).start()
            return carry
        lax.fori_loop(0, n_pages, body, 0, unroll=8)

    @pl.when(b == 0)
    def _():
        issue(0, 0)

    @pl.when(b + 1 < nb)
    def _():
        issue(b + 1, (b + 1) % 2)

    slot = b % 2
    pltpu.make_async_copy(lat_hbm.at[layer, pl.ds(0, n_pages)], latbuf.at[slot], sem.at[0, slot]).wait()
    pltpu.make_async_copy(pet_hbm.at[layer, pl.ds(0, n_pages)], pebuf.at[slot], sem.at[1, slot]).wait()

    ql = ql_ref[0]
    qp = qp_ref[0]
    qlb = ql.astype(BF16)
    qpb = qp.astype(BF16)
    rows = pages_per_chunk * PAGE_SIZE
    for ci in range(n_pages // pages_per_chunk):
        p0 = ci * pages_per_chunk
        latc = latbuf[slot, p0:p0 + pages_per_chunk].reshape(rows, KV_RANK).astype(BF16)
        latbf[ci * rows:(ci + 1) * rows, :] = latc
        pec = jnp.concatenate([pebuf[slot, p0 + p] for p in range(pages_per_chunk)], axis=1).astype(BF16)
        s_ref[:, ci * rows:(ci + 1) * rows] = _dot_nt(qlb, latc) + _dot(qpb, pec)
    s = s_ref[...]
    cn = cn_ref[0]
    pn = pn_ref[0]
    s_new = jnp.sum(ql * cn, axis=-1, keepdims=True) + jnp.sum(qp * pn, axis=-1, keepdims=True)
    m = jnp.maximum(jnp.max(s, axis=-1, keepdims=True), s_new)
    p = jnp.exp(s - m)
    p_new = jnp.exp(s_new - m)
    l = jnp.sum(p, axis=-1, keepdims=True) + p_new
    acc = _dot(p.astype(BF16), latbf[...])
    o_ref[0] = (acc + p_new * cn) / l


def _decode_attention(page_table, ql, qp, lat_new, pe_new, cache_lat, cache_pet, layer):
    B, n_pages = page_table.shape
    H = MLA_HEADS
    ppc = 8
    grid_spec = pltpu.PrefetchScalarGridSpec(
        num_scalar_prefetch=1,
        grid=(B,),
        in_specs=[pl.BlockSpec((1, H, KV_RANK), lambda b, pt: (b, 0, 0)),
                  pl.BlockSpec((1, H, QK_ROPE), lambda b, pt: (b, 0, 0)),
                  pl.BlockSpec((1, 1, KV_RANK), lambda b, pt: (b, 0, 0)),
                  pl.BlockSpec((1, 1, QK_ROPE), lambda b, pt: (b, 0, 0)),
                  pl.BlockSpec(memory_space=pl.ANY),
                  pl.BlockSpec(memory_space=pl.ANY)],
        out_specs=pl.BlockSpec((1, H, KV_RANK), lambda b, pt: (b, 0, 0)),
        scratch_shapes=[pltpu.VMEM((2, n_pages, PAGE_SIZE, KV_RANK), F32),
                        pltpu.VMEM((2, n_pages, QK_ROPE, PAGE_SIZE), F32),
                        pltpu.VMEM((n_pages * PAGE_SIZE, KV_RANK), BF16),
                        pltpu.VMEM((H, n_pages * PAGE_SIZE), F32),
                        pltpu.SemaphoreType.DMA((2, 2))],
    )
    return pl.pallas_call(
        functools.partial(_decode_kernel, layer=layer, n_pages=n_pages, pages_per_chunk=ppc),
        grid_spec=grid_spec,
        out_shape=jax.ShapeDtypeStruct((B, H, KV_RANK), F32),
        compiler_params=_cp("arbitrary"),
        name="mla_decode",
    )(page_table, ql.reshape(B, H, KV_RANK), qp.reshape(B, H, QK_ROPE),
      lat_new.reshape(B, 1, KV_RANK), pe_new.reshape(B, 1, QK_ROPE), cache_lat, cache_pet)


def _uv_kernel(ol_ref, wv_ref, o_ref):
    acc = jnp.zeros(o_ref.shape, F32)
    for h in range(MLA_HEADS):
        acc = acc + _dot(ol_ref[:, h * KV_RANK:(h + 1) * KV_RANK].astype(BF16), wv_ref[h])
    o_ref[...] = acc


def _uv_proj(olat, wvp):
    T = olat.shape[0]
    return pl.pallas_call(
        _uv_kernel,
        out_shape=jax.ShapeDtypeStruct((T, MLA_HEADS * V_HEAD), F32),
        compiler_params=pltpu.CompilerParams(vmem_limit_bytes=VMEM_LIMIT),
        name="mla_uv_sample",
    )(olat, wvp)


def _hgrn_step_kernel(hq_ref, hf_ref, hi_ref, hg_ref, lb_ref, ng_ref, s0_ref, o_ref, s_ref, *, bt):
    ng = ng_ref[...]
    for bi in range(bt):
        for h in range(HG_HEADS):
            sl = slice(h * HG_DK, (h + 1) * HG_DK)
            r = slice(bi, bi + 1)
            lbh = lb_ref[:, sl]
            f = lbh + (1.0 - lbh) * jax.nn.sigmoid(hf_ref[r, sl])
            kk = 1.0 - f
            q = _silu(hq_ref[r, sl]) * (HG_DK ** -0.5)
            v = hi_ref[r, sl]
            fcol = jnp.broadcast_to(f, (HG_DK, HG_DK)).T
            kcol = jnp.broadcast_to(kk, (HG_DK, HG_DK)).T
            s_new = fcol * s0_ref[0, bi, h] + kcol * v
            s_ref[bi, h] = s_new
            o = _dot(jnp.broadcast_to(q, (SUBLANES, HG_DK)).astype(BF16), s_new.astype(BF16))[0:1]
            o_ref[r, sl] = _rms(o, ng) * _silu(hg_ref[r, sl])


def _hgrn_step(z, lb, ng, states, layer):
    B = z.shape[0]
    bt = SUBLANES
    W = HG_WIDTH
    spec = lambda cb: pl.BlockSpec((bt, W), lambda i, cb=cb: (i, cb))
    blk = (bt, HG_HEADS, HG_DK, HG_DV)
    return pl.pallas_call(
        functools.partial(_hgrn_step_kernel, bt=bt),
        grid=(B // bt,),
        in_specs=[spec(0), spec(1), spec(2), spec(3),
                  pl.BlockSpec((1, W), lambda i: (0, 0)), pl.BlockSpec((1, HG_DV), lambda i: (0, 0)),
                  pl.BlockSpec((1,) + blk, lambda i: (layer, i, 0, 0, 0))],
        out_specs=[pl.BlockSpec((bt, W), lambda i: (i, 0)), pl.BlockSpec(blk, lambda i: (i, 0, 0, 0))],
        out_shape=[jax.ShapeDtypeStruct((B, W), F32), jax.ShapeDtypeStruct(states.shape[1:], F32)],
        compiler_params=_cp("parallel"),
        name="hgrn_sample",
    )(z, z, z, z, lb, ng, states)


def _conv_step_kernel(glu_ref, st_ref, w_ref, bias_ref, lg_ref, lb_ref, o_ref, ns_ref):
    K = CONV_WIDTH - 1
    g = glu_ref[...]
    u = g[:, :CONV_CH] * jax.nn.sigmoid(g[:, CONV_CH:])
    y = bias_ref[...] + w_ref[K:K + 1, :] * u
    for j in range(K):
        y = y + w_ref[j:j + 1, :] * st_ref[0, j]
    o_ref[...] = _silu(_layernorm(y, lg_ref[...], lb_ref[...]))
    for j in range(K - 1):
        ns_ref[j] = st_ref[0, j + 1]
    ns_ref[K - 1] = u


def _conv_step(z, states_t, w, bias, lg, lb, layer):
    B = z.shape[0]
    bt = min(B, 32)
    K = CONV_WIDTH - 1
    gblk = HG_SEG // GLU_SEG
    const = lambda i: (0, 0)
    return pl.pallas_call(
        _conv_step_kernel,
        grid=(B // bt,),
        in_specs=[pl.BlockSpec((bt, GLU_SEG), lambda i: (i, gblk)),
                  pl.BlockSpec((1, K, bt, CONV_CH), lambda i: (layer, 0, i, 0)),
                  pl.BlockSpec((CONV_WIDTH, CONV_CH), const), pl.BlockSpec((1, CONV_CH), const),
                  pl.BlockSpec((1, CONV_CH), const), pl.BlockSpec((1, CONV_CH), const)],
        out_specs=[pl.BlockSpec((bt, CONV_CH), lambda i: (i, 0)),
                   pl.BlockSpec((K, bt, CONV_CH), lambda i: (0, i, 0))],
        out_shape=[jax.ShapeDtypeStruct((B, CONV_CH), F32), jax.ShapeDtypeStruct((K, B, CONV_CH), F32)],
        compiler_params=_cp("parallel"),
        name="conv_sample",
    )(z, states_t, w, bias, lg, lb)


def _rope_tables(pos):
    half = QK_ROPE // 2
    inv = ROPE_THETA ** (-jnp.arange(half, dtype=F32) / half)
    ang = pos.astype(F32)[:, None] * inv[None, :]
    return jnp.cos(ang), jnp.sin(ang)


def _layer_params(l, w_in, mla_w_uq, mla_w_ukv):
    H = MLA_HEADS
    w = w_in[l]
    o_kv = Q_RANK
    o_pe = o_kv + KV_RANK
    o_hg = o_pe + QK_ROPE
    o_glu = o_hg + HG_SEG
    o_gate = o_glu + GLU_SEG
    kpe = w[:, o_pe:o_hg]
    half = QK_ROPE // 2
    w_z = jnp.concatenate(
        [w[:, o_hg:o_glu], w[:, o_glu:o_gate], w[:, o_gate:], w[:, :o_pe], kpe,
         kpe[:, half:], kpe[:, :half], jnp.zeros((D_MODEL, 64), F32)], axis=1).astype(BF16)

    uq = mla_w_uq[l].reshape(Q_RANK, H, QK_NOPE + QK_ROPE)
    nope, rope = uq[..., :QK_NOPE], uq[..., QK_NOPE:]
    rope_sw = jnp.concatenate([rope[..., half:], rope[..., :half]], axis=-1)
    z32 = jnp.zeros((Q_RANK, H, HEAD_PAD - QK_NOPE - QK_ROPE), F32)
    z64 = jnp.zeros((Q_RANK, H, QK_NOPE), F32)
    wqa = jnp.concatenate([nope, rope, z32], axis=-1).reshape(Q_RANK, H * HEAD_PAD).astype(BF16)
    wqb = jnp.concatenate([z64, rope_sw, z32], axis=-1).reshape(Q_RANK, H * HEAD_PAD).astype(BF16)
    wqn = jnp.concatenate([nope, z64], axis=-1).reshape(Q_RANK, H * HEAD_PAD).astype(BF16)
    wqr = rope.reshape(Q_RANK, H * QK_ROPE).astype(BF16)
    wqs = rope_sw.reshape(Q_RANK, H * QK_ROPE).astype(BF16)

    ukv = mla_w_ukv[l].reshape(KV_RANK, H, QK_NOPE + V_HEAD)
    uk, uv = ukv[..., :QK_NOPE], ukv[..., QK_NOPE:]
    wk = jnp.concatenate([uk, jnp.zeros((KV_RANK, H, HEAD_PAD - QK_NOPE), F32)], axis=-1)
    wk = wk.reshape(KV_RANK, H * HEAD_PAD).astype(BF16)
    wv = uv.reshape(KV_RANK, H * V_HEAD).T.astype(BF16)
    ukt = jnp.transpose(uk, (1, 2, 0))
    wukt = jnp.concatenate([ukt, jnp.zeros((H, HEAD_PAD - QK_NOPE, KV_RANK), F32)], axis=1).astype(BF16)
    head_of_col = jnp.arange(H * V_HEAD) // V_HEAD
    wvp = jnp.where(head_of_col[None, None, :] == jnp.arange(H)[:, None, None],
                    uv.reshape(KV_RANK, H * V_HEAD)[None], 0.0).astype(BF16)
    return dict(w_z=w_z, wqa=wqa, wqb=wqb, wqn=wqn, wqr=wqr, wqs=wqs, wk=wk, wv=wv, wukt=wukt, wvp=wvp)


def kernel(x_prompt, x_sample, cache_kv_latent, cache_k_rope, state_hgrn, state_conv, page_table,
           norm_ffa, w_ffa_gate, w_ffa_up, w_ffa_down, norm_mix, w_in,
           mla_q_norm, mla_w_uq, mla_kv_norm, mla_w_ukv, mla_w_o,
           hg_lower_bounds, hg_norm, hg_w_o,
           conv_w, conv_b, conv_ln_g, conv_ln_b, conv_w_o, w_out,
           norm_ffb, w_ffb_gate, w_ffb_up, w_ffb_down, norm_final):
    B, S, D = x_prompt.shape
    Bs, Ts, _ = x_sample.shape
    assert Ts == 1, "the sample kernels advance exactly one token per sequence"
    past_len = page_table.shape[1] * PAGE_SIZE
    H = MLA_HEADS
    TM = 1024 if (B * S) % 1024 == 0 else S
    TM_MIX = 512 if S % 512 == 0 else S
    TQ = 512 if S % 512 == 0 else S
    TC = 256

    lb_all = _lower_bounds(hg_lower_bounds)

    half = QK_ROPE // 2
    cos_p, sin_p = _rope_tables(jnp.arange(S))
    cos_s, sin_s = _rope_tables(past_len + jnp.arange(Ts))
    ones = jnp.ones((S, QK_NOPE), F32)
    zpad = jnp.zeros((S, HEAD_PAD - QK_NOPE - QK_ROPE), F32)
    cosq_p = jnp.concatenate([ones, cos_p, cos_p, zpad], axis=1) * (MLA_SCALE * LOG2E)
    sinq_p = jnp.concatenate([0.0 * ones, -sin_p, sin_p, zpad], axis=1) * (MLA_SCALE * LOG2E)
    cosk_p = jnp.concatenate([cos_p, cos_p], axis=1)
    sink_p = jnp.concatenate([-sin_p, sin_p], axis=1)
    cosk_s = jnp.concatenate([cos_s, cos_s], axis=1)
    sink_s = jnp.concatenate([-sin_s, sin_s], axis=1)
    cosq_s = jnp.tile(cosk_s, (1, H)) * MLA_SCALE
    sinq_s = jnp.tile(sink_s, (1, H)) * MLA_SCALE
    pk = (jnp.arange(H * HEAD_PAD)[None, :] % HEAD_PAD == QK_NOPE + jnp.arange(QK_ROPE)[:, None]).astype(BF16)

    cache_pet = jnp.swapaxes(cache_k_rope, 2, 3)
    state_conv_t = jnp.transpose(state_conv, (0, 2, 1, 3))

    row = lambda a, l: a[l][None, :]
    bf = lambda a, l: a[l].astype(BF16)

    xp = x_prompt.reshape(B * S, D)
    xs = x_sample.reshape(Bs * Ts, D)
    outs = [[] for _ in range(8)]
    for l in range(DEPTH):
        lp = _layer_params(l, w_in, mla_w_uq, mla_w_ukv)
        ffa = (row(norm_ffa, l), bf(w_ffa_gate, l), bf(w_ffa_up, l), bf(w_ffa_down, l))
        ffb = (row(norm_ffb, l), bf(w_ffb_gate, l), bf(w_ffb_up, l), bf(w_ffb_down, l))
        wo = (bf(mla_w_o, l), bf(hg_w_o, l), bf(conv_w_o, l), bf(w_out, l))
        lb = lb_all[l][None, :]
        ng = row(hg_norm, l)
        cvp = (conv_w[l], row(conv_b, l), row(conv_ln_g, l), row(conv_ln_b, l))

        xp = _ffn(xp, *ffa, tm=TM)
        z = _inproj(xp, row(norm_mix, l), lp["w_z"], tm=TM)
        q, lat, pe, k, v = _mla_prep(z, S, row(mla_q_norm, l), row(mla_kv_norm, l), lp["wqa"], lp["wqb"],
                                     cosq_p, sinq_p, lp["wk"], lp["wv"], pk, cosk_p, sink_p, tm=TM_MIX, tk=TQ)
        att = _attention(q, k, v, B, S, TQ)
        hg, hg_state = _hgrn(z, lb, ng, B, S)
        cv, cv_state = _conv(z, *cvp, B=B, S=S, tc=TC)
        xp = _post(xp, att, hg, cv, z, *wo, tm=TM_MIX)
        xp = _ffn(xp, *ffb, tm=TM)
        outs[0].append(lat.reshape(B, S, KV_RANK))
        outs[1].append(pe.reshape(B, S, QK_ROPE))
        outs[2].append(hg_state)
        outs[3].append(cv_state)

        xs = _ffn(xs, *ffa, tm=Bs)
        zs = _inproj(xs, row(norm_mix, l), lp["w_z"], tm=Bs)
        ql, qp, lat_s, pe_s = _mla_prep_s(zs, row(mla_q_norm, l), row(mla_kv_norm, l), lp["wqn"], lp["wqr"],
                                          lp["wqs"], lp["wukt"], cosq_s, sinq_s, cosk_s, sink_s)
        olat = _decode_attention(page_table, ql, qp, lat_s, pe_s, cache_kv_latent, cache_pet, l)
        att_s = _uv_proj(olat.reshape(Bs, H * KV_RANK), lp["wvp"])
        hg_s, hg_state_s = _hgrn_step(zs, lb, ng, state_hgrn, l)
        cv_s, cv_state_s = _conv_step(zs, state_conv_t, *cvp, layer=l)
        xs = _post(xs, att_s, hg_s, cv_s, zs, *wo, tm=Bs)
        xs = _ffn(xs, *ffb, tm=Bs)
        outs[4].append(lat_s.reshape(Bs, Ts, KV_RANK))
        outs[5].append(pe_s.reshape(Bs, Ts, QK_ROPE))
        outs[6].append(hg_state_s)
        outs[7].append(cv_state_s)

    y_prompt = _final_norm(xp, norm_final[None, :], TM).reshape(B, S, D)
    y_sample = _final_norm(xs, norm_final[None, :], Bs).reshape(Bs, Ts, D)
    stacked = [jnp.stack(o) for o in outs]
    stacked[7] = jnp.transpose(stacked[7], (0, 2, 1, 3))
    return (y_prompt, y_sample) + tuple(stacked)
```

```python
import functools
import math

import jax
import jax.numpy as jnp
from jax import lax
from jax.experimental import pallas as pl
from jax.experimental.pallas import tpu as pltpu

F32 = jnp.float32
BF16 = jnp.bfloat16

D_MODEL = 1024
DEPTH = 4
PAGE_SIZE = 128
MLA_HEADS = 8
QK_NOPE = 64
QK_ROPE = 32
V_HEAD = 64
Q_RANK = 384
KV_RANK = 256
ROPE_THETA = 10000.0
MLA_SCALE = 1.0 / math.sqrt(QK_NOPE + QK_ROPE)
HG_HEADS = 4
HG_DK = 128
HG_DV = 128
HG_WIDTH = HG_HEADS * HG_DK
CONV_CH = 512
CONV_WIDTH = 31
EPS = 1e-6

LANES = 128
SUBLANES = 8
HEAD_PAD = 128
MLA_SEG = Q_RANK + KV_RANK + 2 * QK_ROPE + 64
HG_SEG = 4 * HG_WIDTH
GLU_SEG = 2 * CONV_CH
GATE_SEG = 3 * D_MODEL
VMEM_LIMIT = 56 * 1024 * 1024
NEG = -1e30
LOG2E = 1.4426950408889634
ATTN_HEAD_GROUP = 4
FFN_CHUNK = 256
HG_CHUNK = 128
HG_SEQ_PER_STEP = 2
CONV_PAD = 32


def _cp(*sem):
    return pltpu.CompilerParams(dimension_semantics=sem, vmem_limit_bytes=VMEM_LIMIT)


def _rms(x, g):
    ms = jnp.mean(x * x, axis=-1, keepdims=True)
    return x * lax.rsqrt(ms + EPS) * g


def _silu(x):
    return x * jax.nn.sigmoid(x)


def _dot(a, b):
    return jnp.dot(a, b, preferred_element_type=F32)


def _dot_nt(a, b):
    return lax.dot_general(a, b, (((1,), (1,)), ((), ())), preferred_element_type=F32)


def _ffn_kernel(x_ref, g_ref, wg_ref, wu_ref, wd_ref, o_ref):
    x = x_ref[...]
    h = _rms(x, g_ref[...]).astype(BF16)
    acc = jnp.zeros(x.shape, F32)
    for c in range(wg_ref.shape[0]):
        g = _dot(h, wg_ref[c])
        u = _dot(h, wu_ref[c])
        acc = acc + _dot((_silu(g) * u).astype(BF16), wd_ref[c])
    o_ref[...] = x + 0.5 * acc


def _ffn(x, g, wg, wu, wd, tm):
    T, D = x.shape
    const3 = lambda i: (0, 0, 0)
    return pl.pallas_call(
        _ffn_kernel,
        grid=(T // tm,),
        in_specs=[pl.BlockSpec((tm, D), lambda i: (i, 0)),
                  pl.BlockSpec((1, D), lambda i: (0, 0)),
                  pl.BlockSpec(wg.shape, const3), pl.BlockSpec(wu.shape, const3), pl.BlockSpec(wd.shape, const3)],
        out_specs=pl.BlockSpec((tm, D), lambda i: (i, 0)),
        out_shape=jax.ShapeDtypeStruct((T, D), F32),
        compiler_params=_cp("parallel"),
        name="ffn",
    )(x, g, wg, wu, wd)


def _inproj_kernel(x_ref, g_ref, w_ref, o_ref):
    o_ref[...] = _dot(_rms(x_ref[...], g_ref[...]).astype(BF16), w_ref[...])


def _inproj(x, g, w, tm):
    T, D = x.shape
    N = w.shape[1]
    return pl.pallas_call(
        _inproj_kernel,
        grid=(T // tm,),
        in_specs=[pl.BlockSpec((tm, D), lambda i: (i, 0)),
                  pl.BlockSpec((1, D), lambda i: (0, 0)),
                  pl.BlockSpec((D, N), lambda i: (0, 0))],
        out_specs=pl.BlockSpec((tm, N), lambda i: (i, 0)),
        out_shape=jax.ShapeDtypeStruct((T, N), F32),
        compiler_params=_cp("parallel"),
        name="inproj",
    )(x, g, w)


def _norm_kernel(x_ref, g_ref, o_ref):
    o_ref[...] = _rms(x_ref[...], g_ref[...])


def _final_norm(x, g, tm):
    T, D = x.shape
    return pl.pallas_call(
        _norm_kernel,
        grid=(T // tm,),
        in_specs=[pl.BlockSpec((tm, D), lambda i: (i, 0)), pl.BlockSpec((1, D), lambda i: (0, 0))],
        out_specs=pl.BlockSpec((tm, D), lambda i: (i, 0)),
        out_shape=jax.ShapeDtypeStruct((T, D), F32),
        compiler_params=_cp("parallel"),
        name="final_norm",
    )(x, g)


def _lb_kernel(x_ref, o_ref):
    x = x_ref[...]
    e = jnp.exp(x - jnp.max(x, axis=0, keepdims=True))
    p = e / jnp.sum(e, axis=0, keepdims=True)
    rows = [p[0:1]]
    for l in range(1, x.shape[0]):
        rows.append(rows[-1] + p[l:l + 1])
    c = jnp.concatenate(rows, axis=0)
    o_ref[...] = c - c[0:1]


def _lower_bounds(x):
    return pl.pallas_call(_lb_kernel, out_shape=jax.ShapeDtypeStruct(x.shape, F32), name="hg_lower_bounds")(x)


def _mla_prep_kernel(z_ref, qn_ref, kvn_ref, wqa_ref, wqb_ref, cq_ref, sq_ref, wk_ref, wv_ref, pk_ref,
                     ck_ref, sk_ref, q_ref, lat_ref, pe_ref, k_ref, v_ref):
    z = z_ref[...]
    cq = z[:, :Q_RANK]
    ckv = z[:, Q_RANK:Q_RANK + KV_RANK]
    o = Q_RANK + KV_RANK
    kpe = z[:, o:o + QK_ROPE]
    kpes = z[:, o + QK_ROPE:o + 2 * QK_ROPE]
    qn = _rms(cq, qn_ref[...]).astype(BF16)
    qa = _dot(qn, wqa_ref[...])
    qb = _dot(qn, wqb_ref[...])
    cs = cq_ref[...]
    sn = sq_ref[...]
    for h in range(MLA_HEADS):
        sl = slice(h * HEAD_PAD, (h + 1) * HEAD_PAD)
        q_ref[:, sl] = (qa[:, sl] * cs + qb[:, sl] * sn).astype(BF16)
    lat = _rms(ckv, kvn_ref[...])
    lat_ref[...] = lat
    per = kpe * ck_ref[...] + kpes * sk_ref[...]
    pe_ref[...] = per
    latb = lat.astype(BF16)
    k_ref[...] = (_dot(latb, wk_ref[...]) + _dot(per.astype(BF16), pk_ref[...])).astype(BF16)
    vt = _dot_nt(wv_ref[...], latb).astype(BF16)
    tk = v_ref.shape[2]
    for c in range(v_ref.shape[0]):
        v_ref[c] = vt[:, c * tk:(c + 1) * tk]


def _mla_prep(z, S, qn, kvn, wqa, wqb, cosq, sinq, wk, wv, pk, cosk, sink, tm, tk):
    T = z.shape[0]
    nb = S // tm
    zblk = GATE_SEG // MLA_SEG
    const = lambda i: (0, 0)
    pos = lambda i: (i % nb, 0)
    H = MLA_HEADS
    return pl.pallas_call(
        _mla_prep_kernel,
        grid=(T // tm,),
        in_specs=[pl.BlockSpec((tm, MLA_SEG), lambda i: (i, zblk)),
                  pl.BlockSpec((1, Q_RANK), const), pl.BlockSpec((1, KV_RANK), const),
                  pl.BlockSpec((Q_RANK, H * HEAD_PAD), const), pl.BlockSpec((Q_RANK, H * HEAD_PAD), const),
                  pl.BlockSpec((tm, HEAD_PAD), pos), pl.BlockSpec((tm, HEAD_PAD), pos),
                  pl.BlockSpec((KV_RANK, H * HEAD_PAD), const), pl.BlockSpec((H * V_HEAD, KV_RANK), const),
                  pl.BlockSpec((QK_ROPE, H * HEAD_PAD), const),
                  pl.BlockSpec((tm, QK_ROPE), pos), pl.BlockSpec((tm, QK_ROPE), pos)],
        out_specs=[pl.BlockSpec((tm, H * HEAD_PAD), lambda i: (i, 0)),
                   pl.BlockSpec((tm, KV_RANK), lambda i: (i, 0)),
                   pl.BlockSpec((tm, QK_ROPE), lambda i: (i, 0)),
                   pl.BlockSpec((tm, H * HEAD_PAD), lambda i: (i, 0)),
                   pl.BlockSpec((tm // tk, H * V_HEAD, tk), lambda i: (i, 0, 0))],
        out_shape=[jax.ShapeDtypeStruct((T, H * HEAD_PAD), BF16),
                   jax.ShapeDtypeStruct((T, KV_RANK), F32),
                   jax.ShapeDtypeStruct((T, QK_ROPE), F32),
                   jax.ShapeDtypeStruct((T, H * HEAD_PAD), BF16),
                   jax.ShapeDtypeStruct((T // tk, H * V_HEAD, tk), BF16)],
        compiler_params=_cp("parallel"),
        name="mla_prep",
    )(z, qn, kvn, wqa, wqb, cosq, sinq, wk, wv, pk, cosk, sink)


def _attn_kernel(q_ref, k_ref, v_ref, o_ref, *, tq):
    qi = pl.program_id(1)
    krow = lax.broadcasted_iota(jnp.int32, (tq, tq), 0)
    qcol = lax.broadcasted_iota(jnp.int32, (tq, tq), 1)
    causal = krow <= qcol
    ones_rows = jnp.ones((16, tq), BF16)
    outs = []
    for g0 in range(0, MLA_HEADS, ATTN_HEAD_GROUP):
        heads = tuple(range(g0, g0 + ATTN_HEAD_GROUP))
        qh = [q_ref[:, h * HEAD_PAD:(h + 1) * HEAD_PAD] for h in heads]

        def step(j, carry, masked, heads=heads, qh=qh):
            off = pl.multiple_of(j * tq, tq)
            new = []
            for idx, h in enumerate(heads):
                m, acc = carry[idx]
                k = k_ref[pl.ds(off, tq), h * HEAD_PAD:(h + 1) * HEAD_PAD]
                vt = jnp.concatenate([v_ref[j, h * V_HEAD:(h + 1) * V_HEAD, :], ones_rows], axis=0)
                st = _dot_nt(k, qh[idx])
                if masked:
                    st = jnp.where(causal, st, NEG)
                m_new = jnp.maximum(m, jnp.max(st, axis=0, keepdims=True))
                p = jnp.exp2(st - m_new)
                acc = jnp.exp2(m - m_new) * acc + _dot(vt, p.astype(BF16))
                new.append((m_new, acc))
            return tuple(new)

        init = tuple((jnp.full((1, tq), NEG, F32), jnp.zeros((V_HEAD + 16, tq), F32)) for _ in heads)
        carry = lax.fori_loop(0, qi, lambda j, c, step=step: step(j, c, False), init)
        carry = step(qi, carry, True)
        outs.extend(acc[:V_HEAD] / acc[V_HEAD:V_HEAD + 1] for _, acc in carry)
    for p in range(MLA_HEADS // 2):
        pair = jnp.concatenate([outs[2 * p], outs[2 * p + 1]], axis=0)
        o_ref[:, p * LANES:(p + 1) * LANES] = pair.T.astype(o_ref.dtype)


def _attention(q, k, v, B, S, tq):
    H = MLA_HEADS
    nq = S // tq
    return pl.pallas_call(
        functools.partial(_attn_kernel, tq=tq),
        grid=(B, nq),
        in_specs=[pl.BlockSpec((tq, H * HEAD_PAD), lambda b, i: (b * nq + i, 0)),
                  pl.BlockSpec((S, H * HEAD_PAD), lambda b, i: (b, 0)),
                  pl.BlockSpec((nq, H * V_HEAD, tq), lambda b, i: (b, 0, 0))],
        out_specs=pl.BlockSpec((tq, H * V_HEAD), lambda b, i: (b * nq + i, 0)),
        out_shape=jax.ShapeDtypeStruct((B * S, H * V_HEAD), BF16),
        compiler_params=_cp("parallel", "arbitrary"),
        name="mla_attention",
    )(q, k, v)


def _hgrn_kernel(hq_ref, hf_ref, hi_ref, hg_ref, lb_ref, ng_ref, o_ref, sout_ref,
                 st_ref, a_ref, b_ref, q_ref, k_ref):
    C = HG_CHUNK
    c = pl.program_id(1)

    @pl.when(c == 0)
    def _():
        st_ref[...] = jnp.zeros_like(st_ref)

    row = lax.broadcasted_iota(jnp.int32, (C, HG_DK), 0)
    row_c = lax.broadcasted_iota(jnp.int32, (C, C), 0)
    col_c = lax.broadcasted_iota(jnp.int32, (C, C), 1)
    sub8 = lax.broadcasted_iota(jnp.int32, (SUBLANES, HG_DK), 0)
    lane8 = lax.broadcasted_iota(jnp.int32, (SUBLANES, C), 1)
    ng = ng_ref[...]
    items = [(bi, h) for bi in range(hq_ref.shape[0]) for h in range(HG_HEADS)]
    for it, (bi, h) in enumerate(items):
        sl = slice(h * HG_DK, (h + 1) * HG_DK)
        lbh = lb_ref[:, sl]
        f = lbh + (1.0 - lbh) * jax.nn.sigmoid(hf_ref[bi, :, sl])
        kk = 1.0 - f
        q = _silu(hq_ref[bi, :, sl]) * (HG_DK ** -0.5)
        b = jnp.log(f)
        sh = 1
        while sh < C:
            b = b + jnp.where(row >= sh, pltpu.roll(b, sh, 0), 0.0)
            sh *= 2
        b_ref[it] = b
        q_ref[it] = q
        k_ref[it] = kk

        amat = jnp.zeros((C, C), F32)
        m = SUBLANES
        while 2 * m <= C:
            span = 2 * m
            bm = jnp.concatenate(
                [jnp.broadcast_to(b[st + m - 1:st + m, :], (span, HG_DK)) for st in range(0, C, span)], axis=0)
            e = jnp.exp(-jnp.abs(b - bm))
            second = (row & (span - 1)) >= m
            ql = jnp.where(second, q * e, 0.0).astype(BF16)
            kl = jnp.where(second, 0.0, kk * e).astype(BF16)
            al = _dot_nt(ql, kl)
            if span < C:
                shift = span.bit_length() - 1
                al = jnp.where((row_c >> shift) == (col_c >> shift), al, 0.0)
            amat = amat + al
            m *= 2
        a_ref[it] = amat

    def diag(i, carry):
        r0 = pl.multiple_of(i * SUBLANES, SUBLANES)
        for it in range(len(items)):
            bb = b_ref[it, pl.ds(r0, SUBLANES), :]
            qb = q_ref[it, pl.ds(r0, SUBLANES), :]
            kb = k_ref[it, pl.ds(r0, SUBLANES), :]
            arow = a_ref[it, pl.ds(r0, SUBLANES), :]
            for j in range(SUBLANES):
                d = jnp.where(sub8 >= j, bb - bb[j:j + 1, :], NEG)
                w = qb * kb[j:j + 1, :] * jnp.exp(d)
                colv = jnp.sum(w, axis=-1, keepdims=True)
                arow = jnp.where(lane8 == r0 + j, colv, arow)
            a_ref[it, pl.ds(r0, SUBLANES), :] = arow
        return carry

    lax.fori_loop(0, C // SUBLANES, diag, 0, unroll=2)

    for it, (bi, h) in enumerate(items):
        sl = slice(h * HG_DK, (h + 1) * HG_DK)
        b = b_ref[it]
        q = q_ref[it]
        kk = k_ref[it]
        v = hi_ref[bi, :, sl]
        amat = a_ref[it]
        vb = v.astype(BF16)
        st = st_ref[it]
        o = _dot(amat.astype(BF16), vb) + _dot_nt((q * jnp.exp(b)).astype(BF16), st.astype(BF16))
        bl = b[C - 1:C, :]
        kdec = (kk * jnp.exp(bl - b)).astype(BF16)
        st_new = st * jnp.exp(bl) + _dot(v.T.astype(BF16), kdec)
        st_ref[it] = st_new
        o_ref[bi, :, sl] = (_rms(o, ng) * _silu(hg_ref[bi, :, sl])).astype(o_ref.dtype)

        @pl.when(c == pl.num_programs(1) - 1)
        def _(bi=bi, h=h, st_new=st_new):
            sout_ref[bi, h] = st_new.T


def _hgrn(z, lb, ng, B, S):
    C = HG_CHUNK
    nb = HG_SEQ_PER_STEP if B % HG_SEQ_PER_STEP == 0 else 1
    n = nb * HG_HEADS
    W = HG_WIDTH
    z3 = z.reshape(B, S, z.shape[1])
    spec = lambda cb: pl.BlockSpec((nb, C, W), lambda b, c, cb=cb: (b, c, cb))
    out, state = pl.pallas_call(
        _hgrn_kernel,
        grid=(B // nb, S // C),
        in_specs=[spec(0), spec(1), spec(2), spec(3),
                  pl.BlockSpec((1, W), lambda b, c: (0, 0)),
                  pl.BlockSpec((1, HG_DV), lambda b, c: (0, 0))],
        out_specs=[pl.BlockSpec((nb, C, W), lambda b, c: (b, c, 0)),
                   pl.BlockSpec((nb, HG_HEADS, HG_DK, HG_DV), lambda b, c: (b, 0, 0, 0))],
        out_shape=[jax.ShapeDtypeStruct((B, S, W), BF16),
                   jax.ShapeDtypeStruct((B, HG_HEADS, HG_DK, HG_DV), F32)],
        scratch_shapes=[pltpu.VMEM((n, HG_DV, HG_DK), F32),
                        pltpu.VMEM((n, C, C), F32),
                        pltpu.VMEM((n, C, HG_DK), F32), pltpu.VMEM((n, C, HG_DK), F32),
                        pltpu.VMEM((n, C, HG_DK), F32)],
        compiler_params=_cp("parallel", "arbitrary"),
        name="hgrn_prompt",
    )(z3, z3, z3, z3, lb, ng)
    return out.reshape(B * S, W), state


def _layernorm(y, g, b):
    mu = jnp.mean(y, axis=-1, keepdims=True)
    d = y - mu
    var = jnp.mean(d * d, axis=-1, keepdims=True)
    return d * lax.rsqrt(var + EPS) * g + b


def _conv_kernel(glu_ref, w_ref, bias_ref, lg_ref, lb_ref, o_ref, st_ref, buf_ref, sh_ref, *, tc):
    c = pl.program_id(1)
    P = CONV_PAD
    first = P - (CONV_WIDTH - 1)
    rc = 32
    span = tc + P - SUBLANES

    @pl.when(c == 0)
    def _():
        buf_ref[0:P, :] = jnp.zeros((P, CONV_CH), F32)

    g = glu_ref[...]
    buf_ref[P:P + tc, :] = g[:, :CONV_CH] * jax.nn.sigmoid(g[:, CONV_CH:])
    for s in range(1, SUBLANES):
        sh_ref[s - 1] = buf_ref[s:s + span, :]
    bias = bias_ref[...]
    lg = lg_ref[...]
    lb = lb_ref[...]
    for r in range(0, tc, rc):
        acc = jnp.broadcast_to(bias, (rc, CONV_CH))
        for j in range(CONV_WIDTH):
            base, s = divmod(first + j, SUBLANES)
            lo = base * SUBLANES + r
            win = buf_ref[lo:lo + rc, :] if s == 0 else sh_ref[s - 1, lo:lo + rc, :]
            acc = acc + w_ref[j:j + 1, :] * win
        o_ref[r:r + rc, :] = _silu(_layernorm(acc, lg, lb)).astype(o_ref.dtype)

    @pl.when(c == pl.num_programs(1) - 1)
    def _():
        st_ref[0] = buf_ref[tc + first:tc + P, :]

    buf_ref[0:P, :] = buf_ref[tc:tc + P, :]


def _conv(z, w, bias, lg, lb, B, S, tc):
    nc = S // tc
    const = lambda b, c: (0, 0)
    gblk = HG_SEG // GLU_SEG
    return pl.pallas_call(
        functools.partial(_conv_kernel, tc=tc),
        grid=(B, nc),
        in_specs=[pl.BlockSpec((tc, GLU_SEG), lambda b, c: (b * nc + c, gblk)),
                  pl.BlockSpec((CONV_WIDTH, CONV_CH), const),
                  pl.BlockSpec((1, CONV_CH), const), pl.BlockSpec((1, CONV_CH), const),
                  pl.BlockSpec((1, CONV_CH), const)],
        out_specs=[pl.BlockSpec((tc, CONV_CH), lambda b, c: (b * nc + c, 0)),
                   pl.BlockSpec((1, CONV_WIDTH - 1, CONV_CH), lambda b, c: (b, 0, 0))],
        out_shape=[jax.ShapeDtypeStruct((B * S, CONV_CH), BF16),
                   jax.ShapeDtypeStruct((B, CONV_WIDTH - 1, CONV_CH), F32)],
        scratch_shapes=[pltpu.VMEM((CONV_PAD + tc, CONV_CH), F32),
                        pltpu.VMEM((SUBLANES - 1, tc + CONV_PAD - SUBLANES, CONV_CH), F32)],
        compiler_params=_cp("parallel", "arbitrary"),
        name="conv_prompt",
    )(z, w, bias, lg, lb)


def _post_kernel(x_ref, a_ref, h_ref, c_ref, gt_ref, wmo_ref, who_ref, wco_ref, wout_ref, o_ref):
    D = D_MODEL
    ya = _dot(a_ref[...].astype(BF16), wmo_ref[...])
    yh = _dot(h_ref[...].astype(BF16), who_ref[...])
    yc = _dot(c_ref[...].astype(BF16), wco_ref[...])
    merged = (jax.nn.sigmoid(gt_ref[:, 0:D]) * ya + jax.nn.sigmoid(gt_ref[:, D:2 * D]) * yh
              + jax.nn.sigmoid(gt_ref[:, 2 * D:3 * D]) * yc)
    o_ref[...] = x_ref[...] + _dot(merged.astype(BF16), wout_ref[...])


def _post(x, a, hg, cv, z, wmo, who, wco, wout, tm):
    T, D = x.shape
    gblk = 0
    row = lambda i: (i, 0)
    const = lambda i: (0, 0)
    return pl.pallas_call(
        _post_kernel,
        grid=(T // tm,),
        in_specs=[pl.BlockSpec((tm, D), row), pl.BlockSpec((tm, a.shape[1]), row),
                  pl.BlockSpec((tm, hg.shape[1]), row), pl.BlockSpec((tm, cv.shape[1]), row),
                  pl.BlockSpec((tm, GATE_SEG), lambda i: (i, gblk)),
                  pl.BlockSpec(wmo.shape, const), pl.BlockSpec(who.shape, const),
                  pl.BlockSpec(wco.shape, const), pl.BlockSpec(wout.shape, const)],
        out_specs=pl.BlockSpec((tm, D), row),
        out_shape=jax.ShapeDtypeStruct((T, D), F32),
        compiler_params=_cp("parallel"),
        name="merge_out",
    )(x, a, hg, cv, z, wmo, who, wco, wout)


def _mla_prep_s_kernel(z_ref, qn_ref, kvn_ref, wqn_ref, wqr_ref, wqs_ref, wukt_ref, cq_ref, sq_ref,
                       ck_ref, sk_ref, ql_ref, qp_ref, lat_ref, pe_ref):
    z = z_ref[...]
    cq = z[:, :Q_RANK]
    ckv = z[:, Q_RANK:Q_RANK + KV_RANK]
    o = Q_RANK + KV_RANK
    kpe = z[:, o:o + QK_ROPE]
    kpes = z[:, o + QK_ROPE:o + 2 * QK_ROPE]
    qn = _rms(cq, qn_ref[...]).astype(BF16)
    qnope = (_dot(qn, wqn_ref[...]) * MLA_SCALE).astype(BF16)
    for h in range(MLA_HEADS):
        ql_ref[:, h * KV_RANK:(h + 1) * KV_RANK] = _dot(qnope[:, h * HEAD_PAD:(h + 1) * HEAD_PAD], wukt_ref[h])
    qp_ref[...] = _dot(qn, wqr_ref[...]) * cq_ref[...] + _dot(qn, wqs_ref[...]) * sq_ref[...]
    lat_ref[...] = _rms(ckv, kvn_ref[...])
    pe_ref[...] = kpe * ck_ref[...] + kpes * sk_ref[...]


def _mla_prep_s(z, qn, kvn, wqn, wqr, wqs, wukt, cosq, sinq, cosk, sink):
    T = z.shape[0]
    H = MLA_HEADS
    zblk = GATE_SEG // MLA_SEG
    full = lambda a: pl.BlockSpec(a.shape, lambda i, n=a.ndim: (0,) * n)
    return pl.pallas_call(
        _mla_prep_s_kernel,
        grid=(1,),
        in_specs=[pl.BlockSpec((T, MLA_SEG), lambda i: (0, zblk)),
                  full(qn), full(kvn), full(wqn), full(wqr), full(wqs), full(wukt),
                  full(cosq), full(sinq), full(cosk), full(sink)],
        out_specs=[pl.BlockSpec((T, H * KV_RANK), lambda i: (0, 0)),
                   pl.BlockSpec((T, H * QK_ROPE), lambda i: (0, 0)),
                   pl.BlockSpec((T, KV_RANK), lambda i: (0, 0)),
                   pl.BlockSpec((T, QK_ROPE), lambda i: (0, 0))],
        out_shape=[jax.ShapeDtypeStruct((T, H * KV_RANK), F32),
                   jax.ShapeDtypeStruct((T, H * QK_ROPE), F32),
                   jax.ShapeDtypeStruct((T, KV_RANK), F32),
                   jax.ShapeDtypeStruct((T, QK_ROPE), F32)],
        compiler_params=_cp("arbitrary"),
        name="mla_prep_sample",
    )(z, qn, kvn, wqn, wqr, wqs, wukt, cosq, sinq, cosk, sink)


def _decode_kernel(pt_ref, ql_ref, qp_ref, cn_ref, pn_ref, lat_hbm, pet_hbm, o_ref,
                   latbuf, pebuf, latbf, s_ref, sem, *, layer, n_pages, pages_per_chunk):
    b = pl.program_id(0)
    nb = pl.num_programs(0)

    def issue(bb, slot):
        def body(p, carry):
            pg = pt_ref[bb, p]
            pltpu.make_async_copy(lat_hbm.at[layer, pg], latbuf.at[slot, p], sem.at[0, slot]).start()
            pltpu.make_async_copy(pet_hbm.at[layer, pg], pebuf.at[slot, p], sem.at[1, slot]).start()
            return carry
        lax.fori_loop(0, n_pages, body, 0, unroll=8)

    @pl.when(b == 0)
    def _():
        issue(0, 0)

    @pl.when(b + 1 < nb)
    def _():
        issue(b + 1, (b + 1) % 2)

    slot = b % 2
    pltpu.make_async_copy(lat_hbm.at[layer, pl.ds(0, n_pages)], latbuf.at[slot], sem.at[0, slot]).wait()
    pltpu.make_async_copy(pet_hbm.at[layer, pl.ds(0, n_pages)], pebuf.at[slot], sem.at[1, slot]).wait()

    ql = ql_ref[0]
    qp = qp_ref[0]
    qlb = ql.astype(BF16)
    qpb = qp.astype(BF16)
    rows = pages_per_chunk * PAGE_SIZE
    for ci in range(n_pages // pages_per_chunk):
        p0 = ci * pages_per_chunk
        latc = latbuf[slot, p0:p0 + pages_per_chunk].reshape(rows, KV_RANK).astype(BF16)
        latbf[ci * rows:(ci + 1) * rows, :] = latc
        pec = jnp.concatenate([pebuf[slot, p0 + p] for p in range(pages_per_chunk)], axis=1).astype(BF16)
        s_ref[:, ci * rows:(ci + 1) * rows] = _dot_nt(qlb, latc) + _dot(qpb, pec)
    s = s_ref[...]
    cn = cn_ref[0]
    pn = pn_ref[0]
    s_new = jnp.sum(ql * cn, axis=-1, keepdims=True) + jnp.sum(qp * pn, axis=-1, keepdims=True)
    m = jnp.maximum(jnp.max(s, axis=-1, keepdims=True), s_new)
    p = jnp.exp(s - m)
    p_new = jnp.exp(s_new - m)
    l = jnp.sum(p, axis=-1, keepdims=True) + p_new
    acc = _dot(p.astype(BF16), latbf[...])
    o_ref[0] = (acc + p_new * cn) / l


def _decode_attention(page_table, ql, qp, lat_new, pe_new, cache_lat, cache_pet, layer):
    B, n_pages = page_table.shape
    H = MLA_HEADS
    ppc = 8
    grid_spec = pltpu.PrefetchScalarGridSpec(
        num_scalar_prefetch=1,
        grid=(B,),
        in_specs=[pl.BlockSpec((1, H, KV_RANK), lambda b, pt: (b, 0, 0)),
                  pl.BlockSpec((1, H, QK_ROPE), lambda b, pt: (b, 0, 0)),
                  pl.BlockSpec((1, 1, KV_RANK), lambda b, pt: (b, 0, 0)),
                  pl.BlockSpec((1, 1, QK_ROPE), lambda b, pt: (b, 0, 0)),
                  pl.BlockSpec(memory_space=pl.ANY),
                  pl.BlockSpec(memory_space=pl.ANY)],
        out_specs=pl.BlockSpec((1, H, KV_RANK), lambda b, pt: (b, 0, 0)),
        scratch_shapes=[pltpu.VMEM((2, n_pages, PAGE_SIZE, KV_RANK), F32),
                        pltpu.VMEM((2, n_pages, QK_ROPE, PAGE_SIZE), F32),
                        pltpu.VMEM((n_pages * PAGE_SIZE, KV_RANK), BF16),
                        pltpu.VMEM((H, n_pages * PAGE_SIZE), F32),
                        pltpu.SemaphoreType.DMA((2, 2))],
    )
    return pl.pallas_call(
        functools.partial(_decode_kernel, layer=layer, n_pages=n_pages, pages_per_chunk=ppc),
        grid_spec=grid_spec,
        out_shape=jax.ShapeDtypeStruct((B, H, KV_RANK), F32),
        compiler_params=_cp("arbitrary"),
        name="mla_decode",
    )(page_table, ql.reshape(B, H, KV_RANK), qp.reshape(B, H, QK_ROPE),
      lat_new.reshape(B, 1, KV_RANK), pe_new.reshape(B, 1, QK_ROPE), cache_lat, cache_pet)


def _uv_kernel(ol_ref, wv_ref, o_ref):
    acc = jnp.zeros(o_ref.shape, F32)
    for h in range(MLA_HEADS):
        acc = acc + _dot(ol_ref[:, h * KV_RANK:(h + 1) * KV_RANK].astype(BF16), wv_ref[h])
    o_ref[...] = acc


def _uv_proj(olat, wvp):
    T = olat.shape[0]
    return pl.pallas_call(
        _uv_kernel,
        out_shape=jax.ShapeDtypeStruct((T, MLA_HEADS * V_HEAD), F32),
        compiler_params=pltpu.CompilerParams(vmem_limit_bytes=VMEM_LIMIT),
        name="mla_uv_sample",
    )(olat, wvp)


def _hgrn_step_kernel(hq_ref, hf_ref, hi_ref, hg_ref, lb_ref, ng_ref, s0_ref, o_ref, s_ref, *, bt):
    ng = ng_ref[...]
    for bi in range(bt):
        for h in range(HG_HEADS):
            sl = slice(h * HG_DK, (h + 1) * HG_DK)
            r = slice(bi, bi + 1)
            lbh = lb_ref[:, sl]
            f = lbh + (1.0 - lbh) * jax.nn.sigmoid(hf_ref[r, sl])
            kk = 1.0 - f
            q = _silu(hq_ref[r, sl]) * (HG_DK ** -0.5)
            v = hi_ref[r, sl]
            fcol = jnp.broadcast_to(f, (HG_DK, HG_DK)).T
            kcol = jnp.broadcast_to(kk, (HG_DK, HG_DK)).T
            s_new = fcol * s0_ref[0, bi, h] + kcol * v
            s_ref[bi, h] = s_new
            o = _dot(jnp.broadcast_to(q, (SUBLANES, HG_DK)).astype(BF16), s_new.astype(BF16))[0:1]
            o_ref[r, sl] = _rms(o, ng) * _silu(hg_ref[r, sl])


def _hgrn_step(z, lb, ng, states, layer):
    B = z.shape[0]
    bt = SUBLANES
    W = HG_WIDTH
    spec = lambda cb: pl.BlockSpec((bt, W), lambda i, cb=cb: (i, cb))
    blk = (bt, HG_HEADS, HG_DK, HG_DV)
    return pl.pallas_call(
        functools.partial(_hgrn_step_kernel, bt=bt),
        grid=(B // bt,),
        in_specs=[spec(0), spec(1), spec(2), spec(3),
                  pl.BlockSpec((1, W), lambda i: (0, 0)), pl.BlockSpec((1, HG_DV), lambda i: (0, 0)),
                  pl.BlockSpec((1,) + blk, lambda i: (layer, i, 0, 0, 0))],
        out_specs=[pl.BlockSpec((bt, W), lambda i: (i, 0)), pl.BlockSpec(blk, lambda i: (i, 0, 0, 0))],
        out_shape=[jax.ShapeDtypeStruct((B, W), F32), jax.ShapeDtypeStruct(states.shape[1:], F32)],
        compiler_params=_cp("parallel"),
        name="hgrn_sample",
    )(z, z, z, z, lb, ng, states)


def _conv_step_kernel(glu_ref, st_ref, w_ref, bias_ref, lg_ref, lb_ref, o_ref, ns_ref):
    K = CONV_WIDTH - 1
    g = glu_ref[...]
    u = g[:, :CONV_CH] * jax.nn.sigmoid(g[:, CONV_CH:])
    y = bias_ref[...] + w_ref[K:K + 1, :] * u
    for j in range(K):
        y = y + w_ref[j:j + 1, :] * st_ref[0, j]
    o_ref[...] = _silu(_layernorm(y, lg_ref[...], lb_ref[...]))
    for j in range(K - 1):
        ns_ref[j] = st_ref[0, j + 1]
    ns_ref[K - 1] = u


def _conv_step(z, states_t, w, bias, lg, lb, layer):
    B = z.shape[0]
    bt = min(B, 32)
    K = CONV_WIDTH - 1
    gblk = HG_SEG // GLU_SEG
    const = lambda i: (0, 0)
    return pl.pallas_call(
        _conv_step_kernel,
        grid=(B // bt,),
        in_specs=[pl.BlockSpec((bt, GLU_SEG), lambda i: (i, gblk)),
                  pl.BlockSpec((1, K, bt, CONV_CH), lambda i: (layer, 0, i, 0)),
                  pl.BlockSpec((CONV_WIDTH, CONV_CH), const), pl.BlockSpec((1, CONV_CH), const),
                  pl.BlockSpec((1, CONV_CH), const), pl.BlockSpec((1, CONV_CH), const)],
        out_specs=[pl.BlockSpec((bt, CONV_CH), lambda i: (i, 0)),
                   pl.BlockSpec((K, bt, CONV_CH), lambda i: (0, i, 0))],
        out_shape=[jax.ShapeDtypeStruct((B, CONV_CH), F32), jax.ShapeDtypeStruct((K, B, CONV_CH), F32)],
        compiler_params=_cp("parallel"),
        name="conv_sample",
    )(z, states_t, w, bias, lg, lb)


def _rope_tables(pos):
    half = QK_ROPE // 2
    inv = ROPE_THETA ** (-jnp.arange(half, dtype=F32) / half)
    ang = pos.astype(F32)[:, None] * inv[None, :]
    return jnp.cos(ang), jnp.sin(ang)


def _layer_params(l, w_in, mla_w_uq, mla_w_ukv):
    H = MLA_HEADS
    w = w_in[l]
    o_kv = Q_RANK
    o_pe = o_kv + KV_RANK
    o_hg = o_pe + QK_ROPE
    o_glu = o_hg + HG_SEG
    o_gate = o_glu + GLU_SEG
    kpe = w[:, o_pe:o_hg]
    half = QK_ROPE // 2
    w_za = w[:, o_hg:o_gate].astype(BF16)
    w_zb = jnp.concatenate(
        [w[:, o_gate:], w[:, :o_pe], kpe, kpe[:, half:], kpe[:, :half], jnp.zeros((D_MODEL, 64), F32)],
        axis=1).astype(BF16)

    uq = mla_w_uq[l].reshape(Q_RANK, H, QK_NOPE + QK_ROPE)
    nope, rope = uq[..., :QK_NOPE], uq[..., QK_NOPE:]
    rope_sw = jnp.concatenate([rope[..., half:], rope[..., :half]], axis=-1)
    z32 = jnp.zeros((Q_RANK, H, HEAD_PAD - QK_NOPE - QK_ROPE), F32)
    z64 = jnp.zeros((Q_RANK, H, QK_NOPE), F32)
    wqa = jnp.concatenate([nope, rope, z32], axis=-1).reshape(Q_RANK, H * HEAD_PAD).astype(BF16)
    wqb = jnp.concatenate([z64, rope_sw, z32], axis=-1).reshape(Q_RANK, H * HEAD_PAD).astype(BF16)
    wqn = jnp.concatenate([nope, z64], axis=-1).reshape(Q_RANK, H * HEAD_PAD).astype(BF16)
    wqr = rope.reshape(Q_RANK, H * QK_ROPE).astype(BF16)
    wqs = rope_sw.reshape(Q_RANK, H * QK_ROPE).astype(BF16)

    ukv = mla_w_ukv[l].reshape(KV_RANK, H, QK_NOPE + V_HEAD)
    uk, uv = ukv[..., :QK_NOPE], ukv[..., QK_NOPE:]
    wk = jnp.concatenate([uk, jnp.zeros((KV_RANK, H, HEAD_PAD - QK_NOPE), F32)], axis=-1)
    wk = wk.reshape(KV_RANK, H * HEAD_PAD).astype(BF16)
    wv = uv.reshape(KV_RANK, H * V_HEAD).T.astype(BF16)
    ukt = jnp.transpose(uk, (1, 2, 0))
    wukt = jnp.concatenate([ukt, jnp.zeros((H, HEAD_PAD - QK_NOPE, KV_RANK), F32)], axis=1).astype(BF16)
    head_of_col = jnp.arange(H * V_HEAD) // V_HEAD
    wvp = jnp.where(head_of_col[None, None, :] == jnp.arange(H)[:, None, None],
                    uv.reshape(KV_RANK, H * V_HEAD)[None], 0.0).astype(BF16)
    return dict(w_za=w_za, w_zb=w_zb, wqa=wqa, wqb=wqb, wqn=wqn, wqr=wqr, wqs=wqs, wk=wk, wv=wv, wukt=wukt, wvp=wvp)


def kernel(x_prompt, x_sample, cache_kv_latent, cache_k_rope, state_hgrn, state_conv, page_table,
           norm_ffa, w_ffa_gate, w_ffa_up, w_ffa_down, norm_mix, w_in,
           mla_q_norm, mla_w_uq, mla_kv_norm, mla_w_ukv, mla_w_o,
           hg_lower_bounds, hg_norm, hg_w_o,
           conv_w, conv_b, conv_ln_g, conv_ln_b, conv_w_o, w_out,
           norm_ffb, w_ffb_gate, w_ffb_up, w_ffb_down, norm_final):
    B, S, D = x_prompt.shape
    Bs, Ts, _ = x_sample.shape
    assert Ts == 1, "the sample kernels advance exactly one token per sequence"
    past_len = page_table.shape[1] * PAGE_SIZE
    H = MLA_HEADS
    TM = 1024 if (B * S) % 1024 == 0 else S
    TM_MIX = 512 if S % 512 == 0 else S
    TQ = 512 if S % 512 == 0 else S
    TC = 256

    lb_all = _lower_bounds(hg_lower_bounds)

    half = QK_ROPE // 2
    cos_p, sin_p = _rope_tables(jnp.arange(S))
    cos_s, sin_s = _rope_tables(past_len + jnp.arange(Ts))
    ones = jnp.ones((S, QK_NOPE), F32)
    zpad = jnp.zeros((S, HEAD_PAD - QK_NOPE - QK_ROPE), F32)
    cosq_p = jnp.concatenate([ones, cos_p, cos_p, zpad], axis=1) * (MLA_SCALE * LOG2E)
    sinq_p = jnp.concatenate([0.0 * ones, -sin_p, sin_p, zpad], axis=1) * (MLA_SCALE * LOG2E)
    cosk_p = jnp.concatenate([cos_p, cos_p], axis=1)
    sink_p = jnp.concatenate([-sin_p, sin_p], axis=1)
    cosk_s = jnp.concatenate([cos_s, cos_s], axis=1)
    sink_s = jnp.concatenate([-sin_s, sin_s], axis=1)
    cosq_s = jnp.tile(cosk_s, (1, H)) * MLA_SCALE
    sinq_s = jnp.tile(sink_s, (1, H)) * MLA_SCALE
    pk = (jnp.arange(H * HEAD_PAD)[None, :] % HEAD_PAD == QK_NOPE + jnp.arange(QK_ROPE)[:, None]).astype(BF16)

    cache_pet = jnp.swapaxes(cache_k_rope, 2, 3)
    state_conv_t = jnp.transpose(state_conv, (0, 2, 1, 3))

    row = lambda a, l: a[l][None, :]
    bf = lambda a, l: a[l].astype(BF16)
    n_ff = w_ffa_gate.shape[2] // FFN_CHUNK
    ff_in = lambda a, l: jnp.transpose(a[l].astype(BF16).reshape(D, n_ff, FFN_CHUNK), (1, 0, 2))
    ff_out = lambda a, l: a[l].astype(BF16).reshape(n_ff, FFN_CHUNK, D)

    xp = x_prompt.reshape(B * S, D)
    xs = x_sample.reshape(Bs * Ts, D)
    outs = [[] for _ in range(8)]
    for l in range(DEPTH):
        lp = _layer_params(l, w_in, mla_w_uq, mla_w_ukv)
        ffa = (row(norm_ffa, l), ff_in(w_ffa_gate, l), ff_in(w_ffa_up, l), ff_out(w_ffa_down, l))
        ffb = (row(norm_ffb, l), ff_in(w_ffb_gate, l), ff_in(w_ffb_up, l), ff_out(w_ffb_down, l))
        wo = (bf(mla_w_o, l), bf(hg_w_o, l), bf(conv_w_o, l), bf(w_out, l))
        lb = lb_all[l][None, :]
        ng = row(hg_norm, l)
        cvp = (conv_w[l], row(conv_b, l), row(conv_ln_g, l), row(conv_ln_b, l))

        xp = _ffn(xp, *ffa, tm=TM_MIX)
        za = _inproj(xp, row(norm_mix, l), lp["w_za"], tm=TM_MIX)
        zb = _inproj(xp, row(norm_mix, l), lp["w_zb"], tm=TM_MIX)
        q, lat, pe, k, v = _mla_prep(zb, S, row(mla_q_norm, l), row(mla_kv_norm, l), lp["wqa"], lp["wqb"],
                                     cosq_p, sinq_p, lp["wk"], lp["wv"], pk, cosk_p, sink_p, tm=TM_MIX, tk=TQ)
        att = _attention(q, k, v, B, S, TQ)
        hg, hg_state = _hgrn(za, lb, ng, B, S)
        cv, cv_state = _conv(za, *cvp, B=B, S=S, tc=TC)
        xp = _post(xp, att, hg, cv, zb, *wo, tm=TM_MIX)
        xp = _ffn(xp, *ffb, tm=TM_MIX)
        outs[0].append(lat.reshape(B, S, KV_RANK))
        outs[1].append(pe.reshape(B, S, QK_ROPE))
        outs[2].append(hg_state)
        outs[3].append(cv_state)

        xs = _ffn(xs, *ffa, tm=Bs)
        zs = _inproj(xs, row(norm_mix, l), lp["w_za"], tm=Bs)
        zsb = _inproj(xs, row(norm_mix, l), lp["w_zb"], tm=Bs)
        ql, qp, lat_s, pe_s = _mla_prep_s(zsb, row(mla_q_norm, l), row(mla_kv_norm, l), lp["wqn"], lp["wqr"],
                                          lp["wqs"], lp["wukt"], cosq_s, sinq_s, cosk_s, sink_s)
        olat = _decode_attention(page_table, ql, qp, lat_s, pe_s, cache_kv_latent, cache_pet, l)
        att_s = _uv_proj(olat.reshape(Bs, H * KV_RANK), lp["wvp"])
        hg_s, hg_state_s = _hgrn_step(zs, lb, ng, state_hgrn, l)
        cv_s, cv_state_s = _conv_step(zs, state_conv_t, *cvp, layer=l)
        xs = _post(xs, att_s, hg_s, cv_s, zsb, *wo, tm=Bs)
        xs = _ffn(xs, *ffb, tm=Bs)
        outs[4].append(lat_s.reshape(Bs, Ts, KV_RANK))
        outs[5].append(pe_s.reshape(Bs, Ts, QK_ROPE))
        outs[6].append(hg_state_s)
        outs[7].append(cv_state_s)

    y_prompt = _final_norm(xp, norm_final[None, :], TM).reshape(B, S, D)
    y_sample = _final_norm(xs, norm_final[None, :], Bs).reshape(Bs, Ts, D)
    stacked = [jnp.stack(o) for o in outs]
    stacked[7] = jnp.transpose(stacked[7], (0, 2, 1, 3))
    return (y_prompt, y_sample) + tuple(stacked)
```

```python
import functools
import math

import jax
import jax.numpy as jnp
from jax import lax
from jax.experimental import pallas as pl
from jax.experimental.pallas import tpu as pltpu

F32 = jnp.float32
BF16 = jnp.bfloat16

D_MODEL = 1024
DEPTH = 4
PAGE_SIZE = 128
MLA_HEADS = 8
QK_NOPE = 64
QK_ROPE = 32
V_HEAD = 64
Q_RANK = 384
KV_RANK = 256
ROPE_THETA = 10000.0
MLA_SCALE = 1.0 / math.sqrt(QK_NOPE + QK_ROPE)
HG_HEADS = 4
HG_DK = 128
HG_DV = 128
HG_WIDTH = HG_HEADS * HG_DK
CONV_CH = 512
CONV_WIDTH = 31
EPS = 1e-6

LANES = 128
SUBLANES = 8
HEAD_PAD = 128
MLA_SEG = Q_RANK + KV_RANK + 2 * QK_ROPE + 64
HG_SEG = 4 * HG_WIDTH
GLU_SEG = 2 * CONV_CH
GATE_SEG = 3 * D_MODEL
VMEM_LIMIT = 56 * 1024 * 1024
NEG = -1e30
LOG2E = 1.4426950408889634
ATTN_HEAD_GROUP = 4
FFN_CHUNK = 256
HG_CHUNK = 128
HG_SEQ_PER_STEP = 4
CONV_PAD = 32


def _cp(*sem):
    return pltpu.CompilerParams(dimension_semantics=sem, vmem_limit_bytes=VMEM_LIMIT)


def _rms(x, g):
    ms = jnp.mean(x * x, axis=-1, keepdims=True)
    return x * lax.rsqrt(ms + EPS) * g


def _silu(x):
    return x * jax.nn.sigmoid(x)


def _dot(a, b):
    return jnp.dot(a, b, preferred_element_type=F32)


def _dot_nt(a, b):
    return lax.dot_general(a, b, (((1,), (1,)), ((), ())), preferred_element_type=F32)


def _ffn_kernel(x_ref, g_ref, wg_ref, wu_ref, wd_ref, o_ref):
    x = x_ref[...]
    h = _rms(x, g_ref[...]).astype(BF16)
    acc = jnp.zeros(x.shape, F32)
    for c in range(wg_ref.shape[0]):
        g = _dot(h, wg_ref[c])
        u = _dot(h, wu_ref[c])
        acc = acc + _dot((_silu(g) * u).astype(BF16), wd_ref[c])
    o_ref[...] = x + 0.5 * acc


def _ffn(x, g, wg, wu, wd, tm):
    T, D = x.shape
    const3 = lambda i: (0, 0, 0)
    return pl.pallas_call(
        _ffn_kernel,
        grid=(T // tm,),
        in_specs=[pl.BlockSpec((tm, D), lambda i: (i, 0)),
                  pl.BlockSpec((1, D), lambda i: (0, 0)),
                  pl.BlockSpec(wg.shape, const3), pl.BlockSpec(wu.shape, const3), pl.BlockSpec(wd.shape, const3)],
        out_specs=pl.BlockSpec((tm, D), lambda i: (i, 0)),
        out_shape=jax.ShapeDtypeStruct((T, D), F32),
        compiler_params=_cp("parallel"),
        name="ffn",
    )(x, g, wg, wu, wd)


def _inproj_kernel(x_ref, g_ref, w_ref, *o_refs):
    h = _rms(x_ref[...], g_ref[...]).astype(BF16)
    col = 0
    for o_ref in o_refs:
        n = o_ref.shape[1]
        o_ref[...] = _dot(h, w_ref[:, col:col + n]).astype(o_ref.dtype)
        col += n


def _inproj(x, g, w, tm, outs):
    T, D = x.shape
    N = w.shape[1]
    assert sum(n for n, _ in outs) == N
    return pl.pallas_call(
        _inproj_kernel,
        grid=(T // tm,),
        in_specs=[pl.BlockSpec((tm, D), lambda i: (i, 0)),
                  pl.BlockSpec((1, D), lambda i: (0, 0)),
                  pl.BlockSpec((D, N), lambda i: (0, 0))],
        out_specs=[pl.BlockSpec((tm, n), lambda i: (i, 0)) for n, _ in outs],
        out_shape=[jax.ShapeDtypeStruct((T, n), dt) for n, dt in outs],
        compiler_params=_cp("parallel"),
        name="inproj",
    )(x, g, w)


def _norm_kernel(x_ref, g_ref, o_ref):
    o_ref[...] = _rms(x_ref[...], g_ref[...])


def _final_norm(x, g, tm):
    T, D = x.shape
    return pl.pallas_call(
        _norm_kernel,
        grid=(T // tm,),
        in_specs=[pl.BlockSpec((tm, D), lambda i: (i, 0)), pl.BlockSpec((1, D), lambda i: (0, 0))],
        out_specs=pl.BlockSpec((tm, D), lambda i: (i, 0)),
        out_shape=jax.ShapeDtypeStruct((T, D), F32),
        compiler_params=_cp("parallel"),
        name="final_norm",
    )(x, g)


def _lb_kernel(x_ref, o_ref):
    x = x_ref[...]
    e = jnp.exp(x - jnp.max(x, axis=0, keepdims=True))
    p = e / jnp.sum(e, axis=0, keepdims=True)
    rows = [p[0:1]]
    for l in range(1, x.shape[0]):
        rows.append(rows[-1] + p[l:l + 1])
    c = jnp.concatenate(rows, axis=0)
    o_ref[...] = c - c[0:1]


def _lower_bounds(x):
    return pl.pallas_call(_lb_kernel, out_shape=jax.ShapeDtypeStruct(x.shape, F32), name="hg_lower_bounds")(x)


def _mla_prep_kernel(z_ref, qn_ref, kvn_ref, wqa_ref, wqb_ref, cq_ref, sq_ref, wk_ref, wv_ref, pk_ref,
                     ck_ref, sk_ref, q_ref, lat_ref, pe_ref, k_ref, v_ref):
    z = z_ref[...]
    cq = z[:, :Q_RANK]
    ckv = z[:, Q_RANK:Q_RANK + KV_RANK]
    o = Q_RANK + KV_RANK
    kpe = z[:, o:o + QK_ROPE]
    kpes = z[:, o + QK_ROPE:o + 2 * QK_ROPE]
    qn = _rms(cq, qn_ref[...]).astype(BF16)
    qa = _dot(qn, wqa_ref[...])
    qb = _dot(qn, wqb_ref[...])
    cs = cq_ref[...]
    sn = sq_ref[...]
    for h in range(MLA_HEADS):
        sl = slice(h * HEAD_PAD, (h + 1) * HEAD_PAD)
        q_ref[:, sl] = (qa[:, sl] * cs + qb[:, sl] * sn).astype(BF16)
    lat = _rms(ckv, kvn_ref[...])
    lat_ref[...] = lat
    per = kpe * ck_ref[...] + kpes * sk_ref[...]
    pe_ref[...] = per
    latb = lat.astype(BF16)
    k_ref[...] = (_dot(latb, wk_ref[...]) + _dot(per.astype(BF16), pk_ref[...])).astype(BF16)
    vt = _dot_nt(wv_ref[...], latb).astype(BF16)
    tk = v_ref.shape[2]
    for c in range(v_ref.shape[0]):
        v_ref[c] = vt[:, c * tk:(c + 1) * tk]


def _mla_prep(z, S, qn, kvn, wqa, wqb, cosq, sinq, wk, wv, pk, cosk, sink, tm, tk):
    T = z.shape[0]
    nb = S // tm
    zblk = 0
    const = lambda i: (0, 0)
    pos = lambda i: (i % nb, 0)
    H = MLA_HEADS
    return pl.pallas_call(
        _mla_prep_kernel,
        grid=(T // tm,),
        in_specs=[pl.BlockSpec((tm, MLA_SEG), lambda i: (i, zblk)),
                  pl.BlockSpec((1, Q_RANK), const), pl.BlockSpec((1, KV_RANK), const),
                  pl.BlockSpec((Q_RANK, H * HEAD_PAD), const), pl.BlockSpec((Q_RANK, H * HEAD_PAD), const),
                  pl.BlockSpec((tm, HEAD_PAD), pos), pl.BlockSpec((tm, HEAD_PAD), pos),
                  pl.BlockSpec((KV_RANK, H * HEAD_PAD), const), pl.BlockSpec((H * V_HEAD, KV_RANK), const),
                  pl.BlockSpec((QK_ROPE, H * HEAD_PAD), const),
                  pl.BlockSpec((tm, QK_ROPE), pos), pl.BlockSpec((tm, QK_ROPE), pos)],
        out_specs=[pl.BlockSpec((tm, H * HEAD_PAD), lambda i: (i, 0)),
                   pl.BlockSpec((tm, KV_RANK), lambda i: (i, 0)),
                   pl.BlockSpec((tm, QK_ROPE), lambda i: (i, 0)),
                   pl.BlockSpec((tm, H * HEAD_PAD), lambda i: (i, 0)),
                   pl.BlockSpec((tm // tk, H * V_HEAD, tk), lambda i: (i, 0, 0))],
        out_shape=[jax.ShapeDtypeStruct((T, H * HEAD_PAD), BF16),
                   jax.ShapeDtypeStruct((T, KV_RANK), F32),
                   jax.ShapeDtypeStruct((T, QK_ROPE), F32),
                   jax.ShapeDtypeStruct((T, H * HEAD_PAD), BF16),
                   jax.ShapeDtypeStruct((T // tk, H * V_HEAD, tk), BF16)],
        compiler_params=_cp("parallel"),
        name="mla_prep",
    )(z, qn, kvn, wqa, wqb, cosq, sinq, wk, wv, pk, cosk, sink)


def _attn_kernel(q_ref, k_ref, v_ref, o_ref, *, tq):
    qi = pl.program_id(1)
    krow = lax.broadcasted_iota(jnp.int32, (tq, tq), 0)
    qcol = lax.broadcasted_iota(jnp.int32, (tq, tq), 1)
    causal = krow <= qcol
    ones_rows = jnp.ones((16, tq), BF16)
    outs = []
    for g0 in range(0, MLA_HEADS, ATTN_HEAD_GROUP):
        heads = tuple(range(g0, g0 + ATTN_HEAD_GROUP))
        qh = [q_ref[:, h * HEAD_PAD:(h + 1) * HEAD_PAD] for h in heads]

        def step(j, carry, masked, heads=heads, qh=qh):
            off = pl.multiple_of(j * tq, tq)
            new = []
            for idx, h in enumerate(heads):
                m, acc = carry[idx]
                k = k_ref[pl.ds(off, tq), h * HEAD_PAD:(h + 1) * HEAD_PAD]
                vt = jnp.concatenate([v_ref[j, h * V_HEAD:(h + 1) * V_HEAD, :], ones_rows], axis=0)
                st = _dot_nt(k, qh[idx])
                if masked:
                    st = jnp.where(causal, st, NEG)
                m_new = jnp.maximum(m, jnp.max(st, axis=0, keepdims=True))
                p = jnp.exp2(st - m_new)
                acc = jnp.exp2(m - m_new) * acc + _dot(vt, p.astype(BF16))
                new.append((m_new, acc))
            return tuple(new)

        init = tuple((jnp.full((1, tq), NEG, F32), jnp.zeros((V_HEAD + 16, tq), F32)) for _ in heads)
        carry = lax.fori_loop(0, qi, lambda j, c, step=step: step(j, c, False), init)
        carry = step(qi, carry, True)
        outs.extend(acc[:V_HEAD] / acc[V_HEAD:V_HEAD + 1] for _, acc in carry)
    for p in range(MLA_HEADS // 2):
        pair = jnp.concatenate([outs[2 * p], outs[2 * p + 1]], axis=0)
        o_ref[:, p * LANES:(p + 1) * LANES] = pair.T.astype(o_ref.dtype)


def _attention(q, k, v, B, S, tq):
    H = MLA_HEADS
    nq = S // tq
    return pl.pallas_call(
        functools.partial(_attn_kernel, tq=tq),
        grid=(B, nq),
        in_specs=[pl.BlockSpec((tq, H * HEAD_PAD), lambda b, i: (b * nq + i, 0)),
                  pl.BlockSpec((S, H * HEAD_PAD), lambda b, i: (b, 0)),
                  pl.BlockSpec((nq, H * V_HEAD, tq), lambda b, i: (b, 0, 0))],
        out_specs=pl.BlockSpec((tq, H * V_HEAD), lambda b, i: (b * nq + i, 0)),
        out_shape=jax.ShapeDtypeStruct((B * S, H * V_HEAD), BF16),
        compiler_params=_cp("parallel", "arbitrary"),
        name="mla_attention",
    )(q, k, v)


def _hgrn_kernel(hq_ref, hf_ref, hi_ref, hg_ref, lb_ref, ng_ref, o_ref, sout_ref,
                 st_ref, a_ref, b_ref, q_ref, k_ref):
    C = HG_CHUNK
    c = pl.program_id(1)

    @pl.when(c == 0)
    def _():
        st_ref[...] = jnp.zeros_like(st_ref)

    row = lax.broadcasted_iota(jnp.int32, (C, HG_DK), 0)
    row_c = lax.broadcasted_iota(jnp.int32, (C, C), 0)
    col_c = lax.broadcasted_iota(jnp.int32, (C, C), 1)
    sub8 = lax.broadcasted_iota(jnp.int32, (SUBLANES, HG_DK), 0)
    lane8 = lax.broadcasted_iota(jnp.int32, (SUBLANES, C), 1)
    ng = ng_ref[...]
    items = [(bi, h) for bi in range(hq_ref.shape[0]) for h in range(HG_HEADS)]
    for it, (bi, h) in enumerate(items):
        sl = slice(h * HG_DK, (h + 1) * HG_DK)
        lbh = lb_ref[:, sl]
        f = lbh + (1.0 - lbh) * jax.nn.sigmoid(hf_ref[bi, :, sl])
        kk = 1.0 - f
        q = _silu(hq_ref[bi, :, sl]) * (HG_DK ** -0.5)
        b = jnp.log(f)
        sh = 1
        while sh < C:
            b = b + jnp.where(row >= sh, pltpu.roll(b, sh, 0), 0.0)
            sh *= 2
        b_ref[it] = b
        q_ref[it] = q
        k_ref[it] = kk

        amat = jnp.zeros((C, C), F32)
        m = SUBLANES
        while 2 * m <= C:
            span = 2 * m
            bm = jnp.concatenate(
                [jnp.broadcast_to(b[st + m - 1:st + m, :], (span, HG_DK)) for st in range(0, C, span)], axis=0)
            e = jnp.exp(-jnp.abs(b - bm))
            second = (row & (span - 1)) >= m
            ql = jnp.where(second, q * e, 0.0).astype(BF16)
            kl = jnp.where(second, 0.0, kk * e).astype(BF16)
            al = _dot_nt(ql, kl)
            if span < C:
                shift = span.bit_length() - 1
                al = jnp.where((row_c >> shift) == (col_c >> shift), al, 0.0)
            amat = amat + al
            m *= 2
        a_ref[it] = amat

    def diag(i, carry):
        r0 = pl.multiple_of(i * SUBLANES, SUBLANES)
        for it in range(len(items)):
            bb = b_ref[it, pl.ds(r0, SUBLANES), :]
            qb = q_ref[it, pl.ds(r0, SUBLANES), :]
            kb = k_ref[it, pl.ds(r0, SUBLANES), :]
            arow = a_ref[it, pl.ds(r0, SUBLANES), :]
            for j in range(SUBLANES):
                d = jnp.where(sub8 >= j, bb - bb[j:j + 1, :], NEG)
                w = qb * kb[j:j + 1, :] * jnp.exp(d)
                colv = jnp.sum(w, axis=-1, keepdims=True)
                arow = jnp.where(lane8 == r0 + j, colv, arow)
            a_ref[it, pl.ds(r0, SUBLANES), :] = arow
        return carry

    lax.fori_loop(0, C // SUBLANES, diag, 0, unroll=2)

    for it, (bi, h) in enumerate(items):
        sl = slice(h * HG_DK, (h + 1) * HG_DK)
        b = b_ref[it]
        q = q_ref[it]
        kk = k_ref[it]
        v = hi_ref[bi, :, sl]
        amat = a_ref[it]
        vb = v.astype(BF16)
        st = st_ref[it]
        o = _dot(amat.astype(BF16), vb) + _dot_nt((q * jnp.exp(b)).astype(BF16), st.astype(BF16))
        bl = b[C - 1:C, :]
        kdec = (kk * jnp.exp(bl - b)).astype(BF16)
        st_new = st * jnp.exp(bl) + _dot(v.T.astype(BF16), kdec)
        st_ref[it] = st_new
        o_ref[bi, :, sl] = (_rms(o, ng) * _silu(hg_ref[bi, :, sl])).astype(o_ref.dtype)

        @pl.when(c == pl.num_programs(1) - 1)
        def _(bi=bi, h=h, st_new=st_new):
            sout_ref[bi, h] = st_new.T


def _hgrn(z, lb, ng, B, S):
    C = HG_CHUNK
    nb = HG_SEQ_PER_STEP if B % HG_SEQ_PER_STEP == 0 else 1
    n = nb * HG_HEADS
    W = HG_WIDTH
    z3 = z.reshape(B, S, z.shape[1])
    spec = lambda cb: pl.BlockSpec((nb, C, W), lambda b, c, cb=cb: (b, c, cb))
    out, state = pl.pallas_call(
        _hgrn_kernel,
        grid=(B // nb, S // C),
        in_specs=[spec(0), spec(1), spec(2), spec(3),
                  pl.BlockSpec((1, W), lambda b, c: (0, 0)),
                  pl.BlockSpec((1, HG_DV), lambda b, c: (0, 0))],
        out_specs=[pl.BlockSpec((nb, C, W), lambda b, c: (b, c, 0)),
                   pl.BlockSpec((nb, HG_HEADS, HG_DK, HG_DV), lambda b, c: (b, 0, 0, 0))],
        out_shape=[jax.ShapeDtypeStruct((B, S, W), BF16),
                   jax.ShapeDtypeStruct((B, HG_HEADS, HG_DK, HG_DV), F32)],
        scratch_shapes=[pltpu.VMEM((n, HG_DV, HG_DK), F32),
                        pltpu.VMEM((n, C, C), F32),
                        pltpu.VMEM((n, C, HG_DK), F32), pltpu.VMEM((n, C, HG_DK), F32),
                        pltpu.VMEM((n, C, HG_DK), F32)],
        compiler_params=_cp("parallel", "arbitrary"),
        name="hgrn_prompt",
    )(z3, z3, z3, z3, lb, ng)
    return out.reshape(B * S, W), state


def _layernorm(y, g, b):
    mu = jnp.mean(y, axis=-1, keepdims=True)
    d = y - mu
    var = jnp.mean(d * d, axis=-1, keepdims=True)
    return d * lax.rsqrt(var + EPS) * g + b


def _conv_kernel(glu_ref, w_ref, bias_ref, lg_ref, lb_ref, o_ref, st_ref, buf_ref, sh_ref, *, tc):
    c = pl.program_id(1)
    P = CONV_PAD
    first = P - (CONV_WIDTH - 1)
    rc = 32
    span = tc + P - SUBLANES

    @pl.when(c == 0)
    def _():
        buf_ref[0:P, :] = jnp.zeros((P, CONV_CH), F32)

    g = glu_ref[...]
    buf_ref[P:P + tc, :] = g[:, :CONV_CH] * jax.nn.sigmoid(g[:, CONV_CH:])
    for s in range(1, SUBLANES):
        sh_ref[s - 1] = buf_ref[s:s + span, :]
    bias = bias_ref[...]
    lg = lg_ref[...]
    lb = lb_ref[...]
    for r in range(0, tc, rc):
        acc = jnp.broadcast_to(bias, (rc, CONV_CH))
        for j in range(CONV_WIDTH):
            base, s = divmod(first + j, SUBLANES)
            lo = base * SUBLANES + r
            win = buf_ref[lo:lo + rc, :] if s == 0 else sh_ref[s - 1, lo:lo + rc, :]
            acc = acc + w_ref[j:j + 1, :] * win
        o_ref[r:r + rc, :] = _silu(_layernorm(acc, lg, lb)).astype(o_ref.dtype)

    @pl.when(c == pl.num_programs(1) - 1)
    def _():
        st_ref[0] = buf_ref[tc + first:tc + P, :]

    buf_ref[0:P, :] = buf_ref[tc:tc + P, :]


def _conv(z, w, bias, lg, lb, B, S, tc):
    nc = S // tc
    const = lambda b, c: (0, 0)
    gblk = HG_SEG // GLU_SEG
    return pl.pallas_call(
        functools.partial(_conv_kernel, tc=tc),
        grid=(B, nc),
        in_specs=[pl.BlockSpec((tc, GLU_SEG), lambda b, c: (b * nc + c, gblk)),
                  pl.BlockSpec((CONV_WIDTH, CONV_CH), const),
                  pl.BlockSpec((1, CONV_CH), const), pl.BlockSpec((1, CONV_CH), const),
                  pl.BlockSpec((1, CONV_CH), const)],
        out_specs=[pl.BlockSpec((tc, CONV_CH), lambda b, c: (b * nc + c, 0)),
                   pl.BlockSpec((1, CONV_WIDTH - 1, CONV_CH), lambda b, c: (b, 0, 0))],
        out_shape=[jax.ShapeDtypeStruct((B * S, CONV_CH), BF16),
                   jax.ShapeDtypeStruct((B, CONV_WIDTH - 1, CONV_CH), F32)],
        scratch_shapes=[pltpu.VMEM((CONV_PAD + tc, CONV_CH), F32),
                        pltpu.VMEM((SUBLANES - 1, tc + CONV_PAD - SUBLANES, CONV_CH), F32)],
        compiler_params=_cp("parallel", "arbitrary"),
        name="conv_prompt",
    )(z, w, bias, lg, lb)


def _post_kernel(x_ref, a_ref, h_ref, c_ref, gt_ref, wmo_ref, who_ref, wco_ref, wout_ref, o_ref):
    D = D_MODEL
    ya = _dot(a_ref[...].astype(BF16), wmo_ref[...])
    yh = _dot(h_ref[...].astype(BF16), who_ref[...])
    yc = _dot(c_ref[...].astype(BF16), wco_ref[...])
    gate = lambda b: jax.nn.sigmoid(gt_ref[:, b * D:(b + 1) * D].astype(F32))
    merged = gate(0) * ya + gate(1) * yh + gate(2) * yc
    o_ref[...] = x_ref[...] + _dot(merged.astype(BF16), wout_ref[...])


def _post(x, a, hg, cv, z, wmo, who, wco, wout, tm):
    T, D = x.shape
    gblk = 0
    row = lambda i: (i, 0)
    const = lambda i: (0, 0)
    return pl.pallas_call(
        _post_kernel,
        grid=(T // tm,),
        in_specs=[pl.BlockSpec((tm, D), row), pl.BlockSpec((tm, a.shape[1]), row),
                  pl.BlockSpec((tm, hg.shape[1]), row), pl.BlockSpec((tm, cv.shape[1]), row),
                  pl.BlockSpec((tm, GATE_SEG), lambda i: (i, gblk)),
                  pl.BlockSpec(wmo.shape, const), pl.BlockSpec(who.shape, const),
                  pl.BlockSpec(wco.shape, const), pl.BlockSpec(wout.shape, const)],
        out_specs=pl.BlockSpec((tm, D), row),
        out_shape=jax.ShapeDtypeStruct((T, D), F32),
        compiler_params=_cp("parallel"),
        name="merge_out",
    )(x, a, hg, cv, z, wmo, who, wco, wout)


def _mla_prep_s_kernel(z_ref, qn_ref, kvn_ref, wqn_ref, wqr_ref, wqs_ref, wukt_ref, cq_ref, sq_ref,
                       ck_ref, sk_ref, ql_ref, qp_ref, lat_ref, pe_ref):
    z = z_ref[...]
    cq = z[:, :Q_RANK]
    ckv = z[:, Q_RANK:Q_RANK + KV_RANK]
    o = Q_RANK + KV_RANK
    kpe = z[:, o:o + QK_ROPE]
    kpes = z[:, o + QK_ROPE:o + 2 * QK_ROPE]
    qn = _rms(cq, qn_ref[...]).astype(BF16)
    qnope = (_dot(qn, wqn_ref[...]) * MLA_SCALE).astype(BF16)
    for h in range(MLA_HEADS):
        ql_ref[:, h * KV_RANK:(h + 1) * KV_RANK] = _dot(qnope[:, h * HEAD_PAD:(h + 1) * HEAD_PAD], wukt_ref[h])
    qp_ref[...] = _dot(qn, wqr_ref[...]) * cq_ref[...] + _dot(qn, wqs_ref[...]) * sq_ref[...]
    lat_ref[...] = _rms(ckv, kvn_ref[...])
    pe_ref[...] = kpe * ck_ref[...] + kpes * sk_ref[...]


def _mla_prep_s(z, qn, kvn, wqn, wqr, wqs, wukt, cosq, sinq, cosk, sink):
    T = z.shape[0]
    H = MLA_HEADS
    zblk = 0
    full = lambda a: pl.BlockSpec(a.shape, lambda i, n=a.ndim: (0,) * n)
    return pl.pallas_call(
        _mla_prep_s_kernel,
        grid=(1,),
        in_specs=[pl.BlockSpec((T, MLA_SEG), lambda i: (0, zblk)),
                  full(qn), full(kvn), full(wqn), full(wqr), full(wqs), full(wukt),
                  full(cosq), full(sinq), full(cosk), full(sink)],
        out_specs=[pl.BlockSpec((T, H * KV_RANK), lambda i: (0, 0)),
                   pl.BlockSpec((T, H * QK_ROPE), lambda i: (0, 0)),
                   pl.BlockSpec((T, KV_RANK), lambda i: (0, 0)),
                   pl.BlockSpec((T, QK_ROPE), lambda i: (0, 0))],
        out_shape=[jax.ShapeDtypeStruct((T, H * KV_RANK), F32),
                   jax.ShapeDtypeStruct((T, H * QK_ROPE), F32),
                   jax.ShapeDtypeStruct((T, KV_RANK), F32),
                   jax.ShapeDtypeStruct((T, QK_ROPE), F32)],
        compiler_params=_cp("arbitrary"),
        name="mla_prep_sample",
    )(z, qn, kvn, wqn, wqr, wqs, wukt, cosq, sinq, cosk, sink)


def _decode_kernel(pt_ref, ql_ref, qp_ref, cn_ref, pn_ref, lat_hbm, pet_hbm, o_ref,
                   latbuf, pebuf, latbf, s_ref, sem, *, layer, n_pages, pages_per_chunk):
    b = pl.program_id(0)
    nb = pl.num_programs(0)

    def issue(bb, slot):
        def body(p, carry):
            pg = pt_ref[bb, p]
            pltpu.make_async_copy(lat_hbm.at[layer, pg], latbuf.at[slot, p], sem.at[0, slot]).start()
            pltpu.make_async_copy(pet_hbm.at[layer, pg], pebuf.at[slot, p], sem.at[1, slot]).start()
            return carry
        lax.fori_loop(0, n_pages, body, 0, unroll=8)

    @pl.when(b == 0)
    def _():
        issue(0, 0)

    @pl.when(b + 1 < nb)
    def _():
        issue(b + 1, (b + 1) % 2)

    slot = b % 2
    pltpu.make_async_copy(lat_hbm.at[layer, pl.ds(0, n_pages)], latbuf.at[slot], sem.at[0, slot]).wait()
    pltpu.make_async_copy(pet_hbm.at[layer, pl.ds(0, n_pages)], pebuf.at[slot], sem.at[1, slot]).wait()

    ql = ql_ref[0]
    qp = qp_ref[0]
    qlb = ql.astype(BF16)
    qpb = qp.astype(BF16)
    rows = pages_per_chunk * PAGE_SIZE
    for ci in range(n_pages // pages_per_chunk):
        p0 = ci * pages_per_chunk
        latc = latbuf[slot, p0:p0 + pages_per_chunk].reshape(rows, KV_RANK).astype(BF16)
        latbf[ci * rows:(ci + 1) * rows, :] = latc
        pec = jnp.concatenate([pebuf[slot, p0 + p] for p in range(pages_per_chunk)], axis=1).astype(BF16)
        s_ref[:, ci * rows:(ci + 1) * rows] = _dot_nt(qlb, latc) + _dot(qpb, pec)
    s = s_ref[...]
    cn = cn_ref[0]
    pn = pn_ref[0]
    s_new = jnp.sum(ql * cn, axis=-1, keepdims=True) + jnp.sum(qp * pn, axis=-1, keepdims=True)
    m = jnp.maximum(jnp.max(s, axis=-1, keepdims=True), s_new)
    p = jnp.exp(s - m)
    p_new = jnp.exp(s_new - m)
    l = jnp.sum(p, axis=-1, keepdims=True) + p_new
    acc = _dot(p.astype(BF16), latbf[...])
    o_ref[0] = (acc + p_new * cn) / l


def _decode_attention(page_table, ql, qp, lat_new, pe_new, cache_lat, cache_pet, layer):
    B, n_pages = page_table.shape
    H = MLA_HEADS
    ppc = 8
    grid_spec = pltpu.PrefetchScalarGridSpec(
        num_scalar_prefetch=1,
        grid=(B,),
        in_specs=[pl.BlockSpec((1, H, KV_RANK), lambda b, pt: (b, 0, 0)),
                  pl.BlockSpec((1, H, QK_ROPE), lambda b, pt: (b, 0, 0)),
                  pl.BlockSpec((1, 1, KV_RANK), lambda b, pt: (b, 0, 0)),
                  pl.BlockSpec((1, 1, QK_ROPE), lambda b, pt: (b, 0, 0)),
                  pl.BlockSpec(memory_space=pl.ANY),
                  pl.BlockSpec(memory_space=pl.ANY)],
        out_specs=pl.BlockSpec((1, H, KV_RANK), lambda b, pt: (b, 0, 0)),
        scratch_shapes=[pltpu.VMEM((2, n_pages, PAGE_SIZE, KV_RANK), F32),
                        pltpu.VMEM((2, n_pages, QK_ROPE, PAGE_SIZE), F32),
                        pltpu.VMEM((n_pages * PAGE_SIZE, KV_RANK), BF16),
                        pltpu.VMEM((H, n_pages * PAGE_SIZE), F32),
                        pltpu.SemaphoreType.DMA((2, 2))],
    )
    return pl.pallas_call(
        functools.partial(_decode_kernel, layer=layer, n_pages=n_pages, pages_per_chunk=ppc),
        grid_spec=grid_spec,
        out_shape=jax.ShapeDtypeStruct((B, H, KV_RANK), F32),
        compiler_params=_cp("arbitrary"),
        name="mla_decode",
    )(page_table, ql.reshape(B, H, KV_RANK), qp.reshape(B, H, QK_ROPE),
      lat_new.reshape(B, 1, KV_RANK), pe_new.reshape(B, 1, QK_ROPE), cache_lat, cache_pet)


def _uv_kernel(ol_ref, wv_ref, o_ref):
    acc = jnp.zeros(o_ref.shape, F32)
    for h in range(MLA_HEADS):
        acc = acc + _dot(ol_ref[:, h * KV_RANK:(h + 1) * KV_RANK].astype(BF16), wv_ref[h])
    o_ref[...] = acc


def _uv_proj(olat, wvp):
    T = olat.shape[0]
    return pl.pallas_call(
        _uv_kernel,
        out_shape=jax.ShapeDtypeStruct((T, MLA_HEADS * V_HEAD), F32),
        compiler_params=pltpu.CompilerParams(vmem_limit_bytes=VMEM_LIMIT),
        name="mla_uv_sample",
    )(olat, wvp)


def _hgrn_step_kernel(hq_ref, hf_ref, hi_ref, hg_ref, lb_ref, ng_ref, s0_ref, o_ref, s_ref, *, bt):
    ng = ng_ref[...]
    for bi in range(bt):
        for h in range(HG_HEADS):
            sl = slice(h * HG_DK, (h + 1) * HG_DK)
            r = slice(bi, bi + 1)
            lbh = lb_ref[:, sl]
            f = lbh + (1.0 - lbh) * jax.nn.sigmoid(hf_ref[r, sl])
            kk = 1.0 - f
            q = _silu(hq_ref[r, sl]) * (HG_DK ** -0.5)
            v = hi_ref[r, sl]
            fcol = jnp.broadcast_to(f, (HG_DK, HG_DK)).T
            kcol = jnp.broadcast_to(kk, (HG_DK, HG_DK)).T
            s_new = fcol * s0_ref[0, bi, h] + kcol * v
            s_ref[bi, h] = s_new
            o = _dot(jnp.broadcast_to(q, (SUBLANES, HG_DK)).astype(BF16), s_new.astype(BF16))[0:1]
            o_ref[r, sl] = _rms(o, ng) * _silu(hg_ref[r, sl])


def _hgrn_step(z, lb, ng, states, layer):
    B = z.shape[0]
    bt = SUBLANES
    W = HG_WIDTH
    spec = lambda cb: pl.BlockSpec((bt, W), lambda i, cb=cb: (i, cb))
    blk = (bt, HG_HEADS, HG_DK, HG_DV)
    return pl.pallas_call(
        functools.partial(_hgrn_step_kernel, bt=bt),
        grid=(B // bt,),
        in_specs=[spec(0), spec(1), spec(2), spec(3),
                  pl.BlockSpec((1, W), lambda i: (0, 0)), pl.BlockSpec((1, HG_DV), lambda i: (0, 0)),
                  pl.BlockSpec((1,) + blk, lambda i: (layer, i, 0, 0, 0))],
        out_specs=[pl.BlockSpec((bt, W), lambda i: (i, 0)), pl.BlockSpec(blk, lambda i: (i, 0, 0, 0))],
        out_shape=[jax.ShapeDtypeStruct((B, W), F32), jax.ShapeDtypeStruct(states.shape[1:], F32)],
        compiler_params=_cp("parallel"),
        name="hgrn_sample",
    )(z, z, z, z, lb, ng, states)


def _conv_step_kernel(glu_ref, st_ref, w_ref, bias_ref, lg_ref, lb_ref, o_ref, ns_ref):
    K = CONV_WIDTH - 1
    g = glu_ref[...]
    u = g[:, :CONV_CH] * jax.nn.sigmoid(g[:, CONV_CH:])
    y = bias_ref[...] + w_ref[K:K + 1, :] * u
    for j in range(K):
        y = y + w_ref[j:j + 1, :] * st_ref[0, j]
    o_ref[...] = _silu(_layernorm(y, lg_ref[...], lb_ref[...]))
    for j in range(K - 1):
        ns_ref[j] = st_ref[0, j + 1]
    ns_ref[K - 1] = u


def _conv_step(z, states_t, w, bias, lg, lb, layer):
    B = z.shape[0]
    bt = min(B, 32)
    K = CONV_WIDTH - 1
    gblk = HG_SEG // GLU_SEG
    const = lambda i: (0, 0)
    return pl.pallas_call(
        _conv_step_kernel,
        grid=(B // bt,),
        in_specs=[pl.BlockSpec((bt, GLU_SEG), lambda i: (i, gblk)),
                  pl.BlockSpec((1, K, bt, CONV_CH), lambda i: (layer, 0, i, 0)),
                  pl.BlockSpec((CONV_WIDTH, CONV_CH), const), pl.BlockSpec((1, CONV_CH), const),
                  pl.BlockSpec((1, CONV_CH), const), pl.BlockSpec((1, CONV_CH), const)],
        out_specs=[pl.BlockSpec((bt, CONV_CH), lambda i: (i, 0)),
                   pl.BlockSpec((K, bt, CONV_CH), lambda i: (0, i, 0))],
        out_shape=[jax.ShapeDtypeStruct((B, CONV_CH), F32), jax.ShapeDtypeStruct((K, B, CONV_CH), F32)],
        compiler_params=_cp("parallel"),
        name="conv_sample",
    )(z, states_t, w, bias, lg, lb)


def _rope_tables(pos):
    half = QK_ROPE // 2
    inv = ROPE_THETA ** (-jnp.arange(half, dtype=F32) / half)
    ang = pos.astype(F32)[:, None] * inv[None, :]
    return jnp.cos(ang), jnp.sin(ang)


def _layer_params(l, w_in, mla_w_uq, mla_w_ukv):
    H = MLA_HEADS
    w = w_in[l]
    o_kv = Q_RANK
    o_pe = o_kv + KV_RANK
    o_hg = o_pe + QK_ROPE
    o_glu = o_hg + HG_SEG
    o_gate = o_glu + GLU_SEG
    kpe = w[:, o_pe:o_hg]
    half = QK_ROPE // 2
    w_za = w[:, o_hg:o_gate].astype(BF16)
    w_zb = jnp.concatenate(
        [w[:, o_gate:], w[:, :o_pe], kpe, kpe[:, half:], kpe[:, :half], jnp.zeros((D_MODEL, 64), F32)],
        axis=1).astype(BF16)

    uq = mla_w_uq[l].reshape(Q_RANK, H, QK_NOPE + QK_ROPE)
    nope, rope = uq[..., :QK_NOPE], uq[..., QK_NOPE:]
    rope_sw = jnp.concatenate([rope[..., half:], rope[..., :half]], axis=-1)
    z32 = jnp.zeros((Q_RANK, H, HEAD_PAD - QK_NOPE - QK_ROPE), F32)
    z64 = jnp.zeros((Q_RANK, H, QK_NOPE), F32)
    wqa = jnp.concatenate([nope, rope, z32], axis=-1).reshape(Q_RANK, H * HEAD_PAD).astype(BF16)
    wqb = jnp.concatenate([z64, rope_sw, z32], axis=-1).reshape(Q_RANK, H * HEAD_PAD).astype(BF16)
    wqn = jnp.concatenate([nope, z64], axis=-1).reshape(Q_RANK, H * HEAD_PAD).astype(BF16)
    wqr = rope.reshape(Q_RANK, H * QK_ROPE).astype(BF16)
    wqs = rope_sw.reshape(Q_RANK, H * QK_ROPE).astype(BF16)

    ukv = mla_w_ukv[l].reshape(KV_RANK, H, QK_NOPE + V_HEAD)
    uk, uv = ukv[..., :QK_NOPE], ukv[..., QK_NOPE:]
    wk = jnp.concatenate([uk, jnp.zeros((KV_RANK, H, HEAD_PAD - QK_NOPE), F32)], axis=-1)
    wk = wk.reshape(KV_RANK, H * HEAD_PAD).astype(BF16)
    wv = uv.reshape(KV_RANK, H * V_HEAD).T.astype(BF16)
    ukt = jnp.transpose(uk, (1, 2, 0))
    wukt = jnp.concatenate([ukt, jnp.zeros((H, HEAD_PAD - QK_NOPE, KV_RANK), F32)], axis=1).astype(BF16)
    head_of_col = jnp.arange(H * V_HEAD) // V_HEAD
    wvp = jnp.where(head_of_col[None, None, :] == jnp.arange(H)[:, None, None],
                    uv.reshape(KV_RANK, H * V_HEAD)[None], 0.0).astype(BF16)
    return dict(w_za=w_za, w_zb=w_zb, wqa=wqa, wqb=wqb, wqn=wqn, wqr=wqr, wqs=wqs, wk=wk, wv=wv, wukt=wukt, wvp=wvp)


def kernel(x_prompt, x_sample, cache_kv_latent, cache_k_rope, state_hgrn, state_conv, page_table,
           norm_ffa, w_ffa_gate, w_ffa_up, w_ffa_down, norm_mix, w_in,
           mla_q_norm, mla_w_uq, mla_kv_norm, mla_w_ukv, mla_w_o,
           hg_lower_bounds, hg_norm, hg_w_o,
           conv_w, conv_b, conv_ln_g, conv_ln_b, conv_w_o, w_out,
           norm_ffb, w_ffb_gate, w_ffb_up, w_ffb_down, norm_final):
    B, S, D = x_prompt.shape
    Bs, Ts, _ = x_sample.shape
    assert Ts == 1, "the sample kernels advance exactly one token per sequence"
    past_len = page_table.shape[1] * PAGE_SIZE
    H = MLA_HEADS
    TM = 1024 if (B * S) % 1024 == 0 else S
    TM_MIX = 512 if S % 512 == 0 else S
    TQ = 512 if S % 512 == 0 else S
    TC = 256

    lb_all = _lower_bounds(hg_lower_bounds)

    half = QK_ROPE // 2
    cos_p, sin_p = _rope_tables(jnp.arange(S))
    cos_s, sin_s = _rope_tables(past_len + jnp.arange(Ts))
    ones = jnp.ones((S, QK_NOPE), F32)
    zpad = jnp.zeros((S, HEAD_PAD - QK_NOPE - QK_ROPE), F32)
    cosq_p = jnp.concatenate([ones, cos_p, cos_p, zpad], axis=1) * (MLA_SCALE * LOG2E)
    sinq_p = jnp.concatenate([0.0 * ones, -sin_p, sin_p, zpad], axis=1) * (MLA_SCALE * LOG2E)
    cosk_p = jnp.concatenate([cos_p, cos_p], axis=1)
    sink_p = jnp.concatenate([-sin_p, sin_p], axis=1)
    cosk_s = jnp.concatenate([cos_s, cos_s], axis=1)
    sink_s = jnp.concatenate([-sin_s, sin_s], axis=1)
    cosq_s = jnp.tile(cosk_s, (1, H)) * MLA_SCALE
    sinq_s = jnp.tile(sink_s, (1, H)) * MLA_SCALE
    pk = (jnp.arange(H * HEAD_PAD)[None, :] % HEAD_PAD == QK_NOPE + jnp.arange(QK_ROPE)[:, None]).astype(BF16)

    cache_pet = jnp.swapaxes(cache_k_rope, 2, 3)
    state_conv_t = jnp.transpose(state_conv, (0, 2, 1, 3))

    row = lambda a, l: a[l][None, :]
    bf = lambda a, l: a[l].astype(BF16)
    za_outs = ((HG_SEG + GLU_SEG, F32),)
    zb_outs = ((GATE_SEG, BF16), (MLA_SEG, F32))
    n_ff = w_ffa_gate.shape[2] // FFN_CHUNK
    ff_in = lambda a, l: jnp.transpose(a[l].astype(BF16).reshape(D, n_ff, FFN_CHUNK), (1, 0, 2))
    ff_out = lambda a, l: a[l].astype(BF16).reshape(n_ff, FFN_CHUNK, D)

    xp = x_prompt.reshape(B * S, D)
    xs = x_sample.reshape(Bs * Ts, D)
    outs = [[] for _ in range(8)]
    for l in range(DEPTH):
        lp = _layer_params(l, w_in, mla_w_uq, mla_w_ukv)
        ffa = (row(norm_ffa, l), ff_in(w_ffa_gate, l), ff_in(w_ffa_up, l), ff_out(w_ffa_down, l))
        ffb = (row(norm_ffb, l), ff_in(w_ffb_gate, l), ff_in(w_ffb_up, l), ff_out(w_ffb_down, l))
        wo = (bf(mla_w_o, l), bf(hg_w_o, l), bf(conv_w_o, l), bf(w_out, l))
        lb = lb_all[l][None, :]
        ng = row(hg_norm, l)
        cvp = (conv_w[l], row(conv_b, l), row(conv_ln_g, l), row(conv_ln_b, l))

        xp = _ffn(xp, *ffa, tm=TM_MIX)
        za, = _inproj(xp, row(norm_mix, l), lp["w_za"], TM_MIX, za_outs)
        zb, zm = _inproj(xp, row(norm_mix, l), lp["w_zb"], TM_MIX, zb_outs)
        q, lat, pe, k, v = _mla_prep(zm, S, row(mla_q_norm, l), row(mla_kv_norm, l), lp["wqa"], lp["wqb"],
                                     cosq_p, sinq_p, lp["wk"], lp["wv"], pk, cosk_p, sink_p, tm=TM_MIX, tk=TQ)
        att = _attention(q, k, v, B, S, TQ)
        hg, hg_state = _hgrn(za, lb, ng, B, S)
        cv, cv_state = _conv(za, *cvp, B=B, S=S, tc=TC)
        xp = _post(xp, att, hg, cv, zb, *wo, tm=TM_MIX)
        xp = _ffn(xp, *ffb, tm=TM_MIX)
        outs[0].append(lat.reshape(B, S, KV_RANK))
        outs[1].append(pe.reshape(B, S, QK_ROPE))
        outs[2].append(hg_state)
        outs[3].append(cv_state)

        xs = _ffn(xs, *ffa, tm=Bs)
        zs, = _inproj(xs, row(norm_mix, l), lp["w_za"], Bs, za_outs)
        zsb, zsm = _inproj(xs, row(norm_mix, l), lp["w_zb"], Bs, zb_outs)
        ql, qp, lat_s, pe_s = _mla_prep_s(zsm, row(mla_q_norm, l), row(mla_kv_norm, l), lp["wqn"], lp["wqr"],
                                          lp["wqs"], lp["wukt"], cosq_s, sinq_s, cosk_s, sink_s)
        olat = _decode_attention(page_table, ql, qp, lat_s, pe_s, cache_kv_latent, cache_pet, l)
        att_s = _uv_proj(olat.reshape(Bs, H * KV_RANK), lp["wvp"])
        hg_s, hg_state_s = _hgrn_step(zs, lb, ng, state_hgrn, l)
        cv_s, cv_state_s = _conv_step(zs, state_conv_t, *cvp, layer=l)
        xs = _post(xs, att_s, hg_s, cv_s, zsb, *wo, tm=Bs)
        xs = _ffn(xs, *ffb, tm=Bs)
        outs[4].append(lat_s.reshape(Bs, Ts, KV_RANK))
        outs[5].append(pe_s.reshape(Bs, Ts, QK_ROPE))
        outs[6].append(hg_state_s)
        outs[7].append(cv_state_s)

    y_prompt = _final_norm(xp, norm_final[None, :], TM).reshape(B, S, D)
    y_sample = _final_norm(xs, norm_final[None, :], Bs).reshape(Bs, Ts, D)
    stacked = [jnp.stack(o) for o in outs]
    stacked[7] = jnp.transpose(stacked[7], (0, 2, 1, 3))
    return (y_prompt, y_sample) + tuple(stacked)
```

```python
import functools
import math

import jax
import jax.numpy as jnp
from jax import lax
from jax.experimental import pallas as pl
from jax.experimental.pallas import tpu as pltpu

F32 = jnp.float32
BF16 = jnp.bfloat16

D_MODEL = 1024
DEPTH = 4
PAGE_SIZE = 128
MLA_HEADS = 8
QK_NOPE = 64
QK_ROPE = 32
V_HEAD = 64
Q_RANK = 384
KV_RANK = 256
ROPE_THETA = 10000.0
MLA_SCALE = 1.0 / math.sqrt(QK_NOPE + QK_ROPE)
HG_HEADS = 4
HG_DK = 128
HG_DV = 128
HG_WIDTH = HG_HEADS * HG_DK
CONV_CH = 512
CONV_WIDTH = 31
EPS = 1e-6

LANES = 128
SUBLANES = 8
HEAD_PAD = 128
MLA_SEG = Q_RANK + KV_RANK + 2 * QK_ROPE + 64
HG_SEG = 4 * HG_WIDTH
GLU_SEG = 2 * CONV_CH
GATE_SEG = 3 * D_MODEL
VMEM_LIMIT = 56 * 1024 * 1024
NEG = -1e30
LOG2E = 1.4426950408889634
ATTN_HEAD_GROUP = 4
FFN_CHUNK = 256
HG_CHUNK = 128
HG_SEQ_PER_STEP = 4
CONV_PAD = 32


def _cp(*sem):
    return pltpu.CompilerParams(dimension_semantics=sem, vmem_limit_bytes=VMEM_LIMIT)


def _rms(x, g):
    ms = jnp.mean(x * x, axis=-1, keepdims=True)
    return x * lax.rsqrt(ms + EPS) * g


def _silu(x):
    return x * jax.nn.sigmoid(x)


def _dot(a, b):
    return jnp.dot(a, b, preferred_element_type=F32)


def _dot_nt(a, b):
    return lax.dot_general(a, b, (((1,), (1,)), ((), ())), preferred_element_type=F32)


def _ffn_kernel(x_ref, g_ref, wg_ref, wu_ref, wd_ref, o_ref):
    x = x_ref[...]
    h = _rms(x, g_ref[...]).astype(BF16)
    acc = jnp.zeros(x.shape, F32)
    for c in range(wg_ref.shape[0]):
        g = _dot(h, wg_ref[c])
        u = _dot(h, wu_ref[c])
        acc = acc + _dot((_silu(g) * u).astype(BF16), wd_ref[c])
    o_ref[...] = x + 0.5 * acc


def _ffn(x, g, wg, wu, wd, tm):
    T, D = x.shape
    const3 = lambda i: (0, 0, 0)
    return pl.pallas_call(
        _ffn_kernel,
        grid=(T // tm,),
        in_specs=[pl.BlockSpec((tm, D), lambda i: (i, 0)),
                  pl.BlockSpec((1, D), lambda i: (0, 0)),
                  pl.BlockSpec(wg.shape, const3), pl.BlockSpec(wu.shape, const3), pl.BlockSpec(wd.shape, const3)],
        out_specs=pl.BlockSpec((tm, D), lambda i: (i, 0)),
        out_shape=jax.ShapeDtypeStruct((T, D), F32),
        compiler_params=_cp("parallel"),
        name="ffn",
    )(x, g, wg, wu, wd)


def _inproj_kernel(x_ref, g_ref, w_ref, *o_refs):
    h = _rms(x_ref[...], g_ref[...]).astype(BF16)
    col = 0
    for o_ref in o_refs:
        n = o_ref.shape[1]
        o_ref[...] = _dot(h, w_ref[:, col:col + n]).astype(o_ref.dtype)
        col += n


def _inproj(x, g, w, tm, outs):
    T, D = x.shape
    N = w.shape[1]
    assert sum(n for n, _ in outs) == N
    return pl.pallas_call(
        _inproj_kernel,
        grid=(T // tm,),
        in_specs=[pl.BlockSpec((tm, D), lambda i: (i, 0)),
                  pl.BlockSpec((1, D), lambda i: (0, 0)),
                  pl.BlockSpec((D, N), lambda i: (0, 0))],
        out_specs=[pl.BlockSpec((tm, n), lambda i: (i, 0)) for n, _ in outs],
        out_shape=[jax.ShapeDtypeStruct((T, n), dt) for n, dt in outs],
        compiler_params=_cp("parallel"),
        name="inproj",
    )(x, g, w)


def _norm_kernel(x_ref, g_ref, o_ref):
    o_ref[...] = _rms(x_ref[...], g_ref[...])


def _final_norm(x, g, tm):
    T, D = x.shape
    return pl.pallas_call(
        _norm_kernel,
        grid=(T // tm,),
        in_specs=[pl.BlockSpec((tm, D), lambda i: (i, 0)), pl.BlockSpec((1, D), lambda i: (0, 0))],
        out_specs=pl.BlockSpec((tm, D), lambda i: (i, 0)),
        out_shape=jax.ShapeDtypeStruct((T, D), F32),
        compiler_params=_cp("parallel"),
        name="final_norm",
    )(x, g)


def _lb_kernel(x_ref, o_ref):
    x = x_ref[...]
    e = jnp.exp(x - jnp.max(x, axis=0, keepdims=True))
    p = e / jnp.sum(e, axis=0, keepdims=True)
    rows = [p[0:1]]
    for l in range(1, x.shape[0]):
        rows.append(rows[-1] + p[l:l + 1])
    c = jnp.concatenate(rows, axis=0)
    o_ref[...] = c - c[0:1]


def _lower_bounds(x):
    return pl.pallas_call(_lb_kernel, out_shape=jax.ShapeDtypeStruct(x.shape, F32), name="hg_lower_bounds")(x)


def _mla_prep_kernel(z_ref, qn_ref, kvn_ref, wqa_ref, wqb_ref, cq_ref, sq_ref, wk_ref, wv_ref, pk_ref,
                     ck_ref, sk_ref, q_ref, lat_ref, pe_ref, k_ref, v_ref):
    z = z_ref[...]
    cq = z[:, :Q_RANK]
    ckv = z[:, Q_RANK:Q_RANK + KV_RANK]
    o = Q_RANK + KV_RANK
    kpe = z[:, o:o + QK_ROPE]
    kpes = z[:, o + QK_ROPE:o + 2 * QK_ROPE]
    qn = _rms(cq, qn_ref[...]).astype(BF16)
    qa = _dot(qn, wqa_ref[...])
    qb = _dot(qn, wqb_ref[...])
    cs = cq_ref[...]
    sn = sq_ref[...]
    for h in range(MLA_HEADS):
        sl = slice(h * HEAD_PAD, (h + 1) * HEAD_PAD)
        q_ref[:, sl] = (qa[:, sl] * cs + qb[:, sl] * sn).astype(BF16)
    lat = _rms(ckv, kvn_ref[...])
    lat_ref[...] = lat
    per = kpe * ck_ref[...] + kpes * sk_ref[...]
    pe_ref[...] = per
    latb = lat.astype(BF16)
    k_ref[...] = (_dot(latb, wk_ref[...]) + _dot(per.astype(BF16), pk_ref[...])).astype(BF16)
    vt = _dot_nt(wv_ref[...], latb).astype(BF16)
    tk = v_ref.shape[2]
    for c in range(v_ref.shape[0]):
        v_ref[c] = vt[:, c * tk:(c + 1) * tk]


def _mla_prep(z, S, qn, kvn, wqa, wqb, cosq, sinq, wk, wv, pk, cosk, sink, tm, tk):
    T = z.shape[0]
    nb = S // tm
    zblk = 0
    const = lambda i: (0, 0)
    pos = lambda i: (i % nb, 0)
    H = MLA_HEADS
    return pl.pallas_call(
        _mla_prep_kernel,
        grid=(T // tm,),
        in_specs=[pl.BlockSpec((tm, MLA_SEG), lambda i: (i, zblk)),
                  pl.BlockSpec((1, Q_RANK), const), pl.BlockSpec((1, KV_RANK), const),
                  pl.BlockSpec((Q_RANK, H * HEAD_PAD), const), pl.BlockSpec((Q_RANK, H * HEAD_PAD), const),
                  pl.BlockSpec((tm, HEAD_PAD), pos), pl.BlockSpec((tm, HEAD_PAD), pos),
                  pl.BlockSpec((KV_RANK, H * HEAD_PAD), const), pl.BlockSpec((H * V_HEAD, KV_RANK), const),
                  pl.BlockSpec((QK_ROPE, H * HEAD_PAD), const),
                  pl.BlockSpec((tm, QK_ROPE), pos), pl.BlockSpec((tm, QK_ROPE), pos)],
        out_specs=[pl.BlockSpec((tm, H * HEAD_PAD), lambda i: (i, 0)),
                   pl.BlockSpec((tm, KV_RANK), lambda i: (i, 0)),
                   pl.BlockSpec((tm, QK_ROPE), lambda i: (i, 0)),
                   pl.BlockSpec((tm, H * HEAD_PAD), lambda i: (i, 0)),
                   pl.BlockSpec((tm // tk, H * V_HEAD, tk), lambda i: (i, 0, 0))],
        out_shape=[jax.ShapeDtypeStruct((T, H * HEAD_PAD), BF16),
                   jax.ShapeDtypeStruct((T, KV_RANK), F32),
                   jax.ShapeDtypeStruct((T, QK_ROPE), F32),
                   jax.ShapeDtypeStruct((T, H * HEAD_PAD), BF16),
                   jax.ShapeDtypeStruct((T // tk, H * V_HEAD, tk), BF16)],
        compiler_params=_cp("parallel"),
        name="mla_prep",
    )(z, qn, kvn, wqa, wqb, cosq, sinq, wk, wv, pk, cosk, sink)


def _attn_kernel(q_ref, k_ref, v_ref, o_ref, *, tq):
    qi = pl.program_id(1)
    krow = lax.broadcasted_iota(jnp.int32, (tq, tq), 0)
    qcol = lax.broadcasted_iota(jnp.int32, (tq, tq), 1)
    causal = krow <= qcol
    ones_rows = jnp.ones((16, tq), BF16)
    outs = []
    for g0 in range(0, MLA_HEADS, ATTN_HEAD_GROUP):
        heads = tuple(range(g0, g0 + ATTN_HEAD_GROUP))
        qh = [q_ref[:, h * HEAD_PAD:(h + 1) * HEAD_PAD] for h in heads]

        def step(j, carry, masked, heads=heads, qh=qh):
            off = pl.multiple_of(j * tq, tq)
            new = []
            for idx, h in enumerate(heads):
                m, acc = carry[idx]
                k = k_ref[pl.ds(off, tq), h * HEAD_PAD:(h + 1) * HEAD_PAD]
                vt = jnp.concatenate([v_ref[j, h * V_HEAD:(h + 1) * V_HEAD, :], ones_rows], axis=0)
                st = _dot_nt(k, qh[idx])
                if masked:
                    st = jnp.where(causal, st, NEG)
                m_new = jnp.maximum(m, jnp.max(st, axis=0, keepdims=True))
                p = jnp.exp2(st - m_new)
                acc = jnp.exp2(m - m_new) * acc + _dot(vt, p.astype(BF16))
                new.append((m_new, acc))
            return tuple(new)

        init = tuple((jnp.full((1, tq), NEG, F32), jnp.zeros((V_HEAD + 16, tq), F32)) for _ in heads)
        carry = lax.fori_loop(0, qi, lambda j, c, step=step: step(j, c, False), init)
        carry = step(qi, carry, True)
        outs.extend(acc[:V_HEAD] / acc[V_HEAD:V_HEAD + 1] for _, acc in carry)
    for p in range(MLA_HEADS // 2):
        pair = jnp.concatenate([outs[2 * p], outs[2 * p + 1]], axis=0)
        o_ref[:, p * LANES:(p + 1) * LANES] = pair.T.astype(o_ref.dtype)


def _attention(q, k, v, B, S, tq):
    H = MLA_HEADS
    nq = S // tq
    return pl.pallas_call(
        functools.partial(_attn_kernel, tq=tq),
        grid=(B, nq),
        in_specs=[pl.BlockSpec((tq, H * HEAD_PAD), lambda b, i: (b * nq + i, 0)),
                  pl.BlockSpec((S, H * HEAD_PAD), lambda b, i: (b, 0)),
                  pl.BlockSpec((nq, H * V_HEAD, tq), lambda b, i: (b, 0, 0))],
        out_specs=pl.BlockSpec((tq, H * V_HEAD), lambda b, i: (b * nq + i, 0)),
        out_shape=jax.ShapeDtypeStruct((B * S, H * V_HEAD), BF16),
        compiler_params=_cp("parallel", "arbitrary"),
        name="mla_attention",
    )(q, k, v)


def _hgrn_kernel(hq_ref, hf_ref, hi_ref, hg_ref, lb_ref, ng_ref, o_ref, sout_ref,
                 st_ref, a_ref, b_ref, q_ref, k_ref):
    C = HG_CHUNK
    c = pl.program_id(1)

    @pl.when(c == 0)
    def _():
        st_ref[...] = jnp.zeros_like(st_ref)

    row = lax.broadcasted_iota(jnp.int32, (C, HG_DK), 0)
    row_c = lax.broadcasted_iota(jnp.int32, (C, C), 0)
    col_c = lax.broadcasted_iota(jnp.int32, (C, C), 1)
    sub8 = lax.broadcasted_iota(jnp.int32, (SUBLANES, HG_DK), 0)
    lane8 = lax.broadcasted_iota(jnp.int32, (SUBLANES, C), 1)
    ng = ng_ref[...]
    items = [(bi, h) for bi in range(hq_ref.shape[0]) for h in range(HG_HEADS)]
    for it, (bi, h) in enumerate(items):
        sl = slice(h * HG_DK, (h + 1) * HG_DK)
        lbh = lb_ref[:, sl]
        f = lbh + (1.0 - lbh) * jax.nn.sigmoid(hf_ref[bi, :, sl])
        kk = 1.0 - f
        q = _silu(hq_ref[bi, :, sl]) * (HG_DK ** -0.5)
        b = jnp.log(f)
        sh = 1
        while sh < C:
            b = b + jnp.where(row >= sh, pltpu.roll(b, sh, 0), 0.0)
            sh *= 2
        b_ref[it] = b
        q_ref[it] = q
        k_ref[it] = kk

        amat = jnp.zeros((C, C), F32)
        m = SUBLANES
        while 2 * m <= C:
            span = 2 * m
            bm = jnp.concatenate(
                [jnp.broadcast_to(b[st + m - 1:st + m, :], (span, HG_DK)) for st in range(0, C, span)], axis=0)
            e = jnp.exp(-jnp.abs(b - bm))
            second = (row & (span - 1)) >= m
            ql = jnp.where(second, q * e, 0.0).astype(BF16)
            kl = jnp.where(second, 0.0, kk * e).astype(BF16)
            al = _dot_nt(ql, kl)
            if span < C:
                shift = span.bit_length() - 1
                al = jnp.where((row_c >> shift) == (col_c >> shift), al, 0.0)
            amat = amat + al
            m *= 2
        a_ref[it] = amat

    def diag(i, carry):
        r0 = pl.multiple_of(i * SUBLANES, SUBLANES)
        for it in range(len(items)):
            bb = b_ref[it, pl.ds(r0, SUBLANES), :]
            qb = q_ref[it, pl.ds(r0, SUBLANES), :]
            kb = k_ref[it, pl.ds(r0, SUBLANES), :]
            arow = a_ref[it, pl.ds(r0, SUBLANES), :]
            for j in range(SUBLANES):
                d = jnp.where(sub8 >= j, bb - bb[j:j + 1, :], NEG)
                w = qb * kb[j:j + 1, :] * jnp.exp(d)
                colv = jnp.sum(w, axis=-1, keepdims=True)
                arow = jnp.where(lane8 == r0 + j, colv, arow)
            a_ref[it, pl.ds(r0, SUBLANES), :] = arow
        return carry

    lax.fori_loop(0, C // SUBLANES, diag, 0, unroll=2)

    for it, (bi, h) in enumerate(items):
        sl = slice(h * HG_DK, (h + 1) * HG_DK)
        b = b_ref[it]
        q = q_ref[it]
        kk = k_ref[it]
        v = hi_ref[bi, :, sl]
        amat = a_ref[it]
        vb = v.astype(BF16)
        st = st_ref[it]
        o = _dot(amat.astype(BF16), vb) + _dot_nt((q * jnp.exp(b)).astype(BF16), st.astype(BF16))
        bl = b[C - 1:C, :]
        kdec = (kk * jnp.exp(bl - b)).astype(BF16)
        st_new = st * jnp.exp(bl) + _dot(v.T.astype(BF16), kdec)
        st_ref[it] = st_new
        o_ref[bi, :, sl] = (_rms(o, ng) * _silu(hg_ref[bi, :, sl])).astype(o_ref.dtype)

        @pl.when(c == pl.num_programs(1) - 1)
        def _(bi=bi, h=h, st_new=st_new):
            sout_ref[bi, h] = st_new.T


def _hgrn(z, lb, ng, B, S):
    C = HG_CHUNK
    nb = HG_SEQ_PER_STEP if B % HG_SEQ_PER_STEP == 0 else 1
    n = nb * HG_HEADS
    W = HG_WIDTH
    z3 = z.reshape(B, S, z.shape[1])
    spec = lambda cb: pl.BlockSpec((nb, C, W), lambda b, c, cb=cb: (b, c, cb))
    out, state = pl.pallas_call(
        _hgrn_kernel,
        grid=(B // nb, S // C),
        in_specs=[spec(0), spec(1), spec(2), spec(3),
                  pl.BlockSpec((1, W), lambda b, c: (0, 0)),
                  pl.BlockSpec((1, HG_DV), lambda b, c: (0, 0))],
        out_specs=[pl.BlockSpec((nb, C, W), lambda b, c: (b, c, 0)),
                   pl.BlockSpec((nb, HG_HEADS, HG_DK, HG_DV), lambda b, c: (b, 0, 0, 0))],
        out_shape=[jax.ShapeDtypeStruct((B, S, W), BF16),
                   jax.ShapeDtypeStruct((B, HG_HEADS, HG_DK, HG_DV), F32)],
        scratch_shapes=[pltpu.VMEM((n, HG_DV, HG_DK), F32),
                        pltpu.VMEM((n, C, C), F32),
                        pltpu.VMEM((n, C, HG_DK), F32), pltpu.VMEM((n, C, HG_DK), F32),
                        pltpu.VMEM((n, C, HG_DK), F32)],
        compiler_params=_cp("parallel", "arbitrary"),
        name="hgrn_prompt",
    )(z3, z3, z3, z3, lb, ng)
    return out.reshape(B * S, W), state


def _layernorm(y, g, b):
    mu = jnp.mean(y, axis=-1, keepdims=True)
    d = y - mu
    var = jnp.mean(d * d, axis=-1, keepdims=True)
    return d * lax.rsqrt(var + EPS) * g + b


def _conv_kernel(glu_ref, w_ref, bias_ref, lg_ref, lb_ref, o_ref, st_ref, buf_ref, sh_ref, *, tc):
    c = pl.program_id(1)
    P = CONV_PAD
    first = P - (CONV_WIDTH - 1)
    rc = 32
    span = tc + P - SUBLANES

    @pl.when(c == 0)
    def _():
        buf_ref[0:P, :] = jnp.zeros((P, CONV_CH), F32)

    g = glu_ref[...]
    buf_ref[P:P + tc, :] = g[:, :CONV_CH] * jax.nn.sigmoid(g[:, CONV_CH:])
    for s in range(1, SUBLANES):
        sh_ref[s - 1] = buf_ref[s:s + span, :]
    bias = bias_ref[...]
    lg = lg_ref[...]
    lb = lb_ref[...]
    for r in range(0, tc, rc):
        acc = jnp.broadcast_to(bias, (rc, CONV_CH))
        for j in range(CONV_WIDTH):
            base, s = divmod(first + j, SUBLANES)
            lo = base * SUBLANES + r
            win = buf_ref[lo:lo + rc, :] if s == 0 else sh_ref[s - 1, lo:lo + rc, :]
            acc = acc + w_ref[j:j + 1, :] * win
        o_ref[r:r + rc, :] = _silu(_layernorm(acc, lg, lb)).astype(o_ref.dtype)

    @pl.when(c == pl.num_programs(1) - 1)
    def _():
        st_ref[0] = buf_ref[tc + first:tc + P, :]

    buf_ref[0:P, :] = buf_ref[tc:tc + P, :]


def _conv(z, w, bias, lg, lb, B, S, tc):
    nc = S // tc
    const = lambda b, c: (0, 0)
    gblk = HG_SEG // GLU_SEG
    return pl.pallas_call(
        functools.partial(_conv_kernel, tc=tc),
        grid=(B, nc),
        in_specs=[pl.BlockSpec((tc, GLU_SEG), lambda b, c: (b * nc + c, gblk)),
                  pl.BlockSpec((CONV_WIDTH, CONV_CH), const),
                  pl.BlockSpec((1, CONV_CH), const), pl.BlockSpec((1, CONV_CH), const),
                  pl.BlockSpec((1, CONV_CH), const)],
        out_specs=[pl.BlockSpec((tc, CONV_CH), lambda b, c: (b * nc + c, 0)),
                   pl.BlockSpec((1, CONV_WIDTH - 1, CONV_CH), lambda b, c: (b, 0, 0))],
        out_shape=[jax.ShapeDtypeStruct((B * S, CONV_CH), BF16),
                   jax.ShapeDtypeStruct((B, CONV_WIDTH - 1, CONV_CH), F32)],
        scratch_shapes=[pltpu.VMEM((CONV_PAD + tc, CONV_CH), F32),
                        pltpu.VMEM((SUBLANES - 1, tc + CONV_PAD - SUBLANES, CONV_CH), F32)],
        compiler_params=_cp("parallel", "arbitrary"),
        name="conv_prompt",
    )(z, w, bias, lg, lb)


def _post_kernel(x_ref, a_ref, h_ref, c_ref, gt_ref, wmo_ref, who_ref, wco_ref, wout_ref, o_ref):
    D = D_MODEL
    ya = _dot(a_ref[...].astype(BF16), wmo_ref[...])
    yh = _dot(h_ref[...].astype(BF16), who_ref[...])
    yc = _dot(c_ref[...].astype(BF16), wco_ref[...])
    gate = lambda b: jax.nn.sigmoid(gt_ref[:, b * D:(b + 1) * D].astype(F32))
    merged = gate(0) * ya + gate(1) * yh + gate(2) * yc
    o_ref[...] = x_ref[...] + _dot(merged.astype(BF16), wout_ref[...])


def _post(x, a, hg, cv, z, wmo, who, wco, wout, tm):
    T, D = x.shape
    gblk = 0
    row = lambda i: (i, 0)
    const = lambda i: (0, 0)
    return pl.pallas_call(
        _post_kernel,
        grid=(T // tm,),
        in_specs=[pl.BlockSpec((tm, D), row), pl.BlockSpec((tm, a.shape[1]), row),
                  pl.BlockSpec((tm, hg.shape[1]), row), pl.BlockSpec((tm, cv.shape[1]), row),
                  pl.BlockSpec((tm, GATE_SEG), lambda i: (i, gblk)),
                  pl.BlockSpec(wmo.shape, const), pl.BlockSpec(who.shape, const),
                  pl.BlockSpec(wco.shape, const), pl.BlockSpec(wout.shape, const)],
        out_specs=pl.BlockSpec((tm, D), row),
        out_shape=jax.ShapeDtypeStruct((T, D), F32),
        compiler_params=_cp("parallel"),
        name="merge_out",
    )(x, a, hg, cv, z, wmo, who, wco, wout)


def _mla_prep_s_kernel(z_ref, qn_ref, kvn_ref, wqn_ref, wqr_ref, wqs_ref, wukt_ref, cq_ref, sq_ref,
                       ck_ref, sk_ref, ql_ref, qp_ref, lat_ref, pe_ref):
    z = z_ref[...]
    cq = z[:, :Q_RANK]
    ckv = z[:, Q_RANK:Q_RANK + KV_RANK]
    o = Q_RANK + KV_RANK
    kpe = z[:, o:o + QK_ROPE]
    kpes = z[:, o + QK_ROPE:o + 2 * QK_ROPE]
    qn = _rms(cq, qn_ref[...]).astype(BF16)
    qnope = (_dot(qn, wqn_ref[...]) * MLA_SCALE).astype(BF16)
    for h in range(MLA_HEADS):
        ql_ref[:, h * KV_RANK:(h + 1) * KV_RANK] = _dot(qnope[:, h * HEAD_PAD:(h + 1) * HEAD_PAD], wukt_ref[h])
    qp_ref[...] = _dot(qn, wqr_ref[...]) * cq_ref[...] + _dot(qn, wqs_ref[...]) * sq_ref[...]
    lat_ref[...] = _rms(ckv, kvn_ref[...])
    pe_ref[...] = kpe * ck_ref[...] + kpes * sk_ref[...]


def _mla_prep_s(z, qn, kvn, wqn, wqr, wqs, wukt, cosq, sinq, cosk, sink):
    T = z.shape[0]
    H = MLA_HEADS
    zblk = 0
    full = lambda a: pl.BlockSpec(a.shape, lambda i, n=a.ndim: (0,) * n)
    return pl.pallas_call(
        _mla_prep_s_kernel,
        grid=(1,),
        in_specs=[pl.BlockSpec((T, MLA_SEG), lambda i: (0, zblk)),
                  full(qn), full(kvn), full(wqn), full(wqr), full(wqs), full(wukt),
                  full(cosq), full(sinq), full(cosk), full(sink)],
        out_specs=[pl.BlockSpec((T, H * KV_RANK), lambda i: (0, 0)),
                   pl.BlockSpec((T, H * QK_ROPE), lambda i: (0, 0)),
                   pl.BlockSpec((T, KV_RANK), lambda i: (0, 0)),
                   pl.BlockSpec((T, QK_ROPE), lambda i: (0, 0))],
        out_shape=[jax.ShapeDtypeStruct((T, H * KV_RANK), F32),
                   jax.ShapeDtypeStruct((T, H * QK_ROPE), F32),
                   jax.ShapeDtypeStruct((T, KV_RANK), F32),
                   jax.ShapeDtypeStruct((T, QK_ROPE), F32)],
        compiler_params=_cp("arbitrary"),
        name="mla_prep_sample",
    )(z, qn, kvn, wqn, wqr, wqs, wukt, cosq, sinq, cosk, sink)


def _decode_kernel(pt_ref, ql_ref, qp_ref, cn_ref, pn_ref, lat_hbm, pet_hbm, o_ref,
                   latbuf, pebuf, latbf, s_ref, sem, *, layer, n_pages, pages_per_chunk):
    b = pl.program_id(0)
    nb = pl.num_programs(0)

    def issue(bb, slot):
        def body(p, carry):
            pg = pt_ref[bb, p]
            pltpu.make_async_copy(lat_hbm.at[layer, pg], latbuf.at[slot, p], sem.at[0, slot]).start()
            pltpu.make_async_copy(pet_hbm.at[layer, pg], pebuf.at[slot, p], sem.at[1, slot]).start()
            return carry
        lax.fori_loop(0, n_pages, body, 0, unroll=8)

    @pl.when(b == 0)
    def _():
        issue(0, 0)

    @pl.when(b + 1 < nb)
    def _():
        issue(b + 1, (b + 1) % 2)

    slot = b % 2
    pltpu.make_async_copy(lat_hbm.at[layer, pl.ds(0, n_pages)], latbuf.at[slot], sem.at[0, slot]).wait()
    pltpu.make_async_copy(pet_hbm.at[layer, pl.ds(0, n_pages)], pebuf.at[slot], sem.at[1, slot]).wait()

    ql = ql_ref[0]
    qp = qp_ref[0]
    qlb = ql.astype(BF16)
    qpb = qp.astype(BF16)
    rows = pages_per_chunk * PAGE_SIZE
    for ci in range(n_pages // pages_per_chunk):
        p0 = ci * pages_per_chunk
        latc = latbuf[slot, p0:p0 + pages_per_chunk].reshape(rows, KV_RANK).astype(BF16)
        latbf[ci * rows:(ci + 1) * rows, :] = latc
        pec = jnp.concatenate([pebuf[slot, p0 + p] for p in range(pages_per_chunk)], axis=1).astype(BF16)
        s_ref[:, ci * rows:(ci + 1) * rows] = _dot_nt(qlb, latc) + _dot(qpb, pec)
    s = s_ref[...]
    cn = cn_ref[0]
    pn = pn_ref[0]
    s_new = jnp.sum(ql * cn, axis=-1, keepdims=True) + jnp.sum(qp * pn, axis=-1, keepdims=True)
    m = jnp.maximum(jnp.max(s, axis=-1, keepdims=True), s_new)
    p = jnp.exp(s - m)
    p_new = jnp.exp(s_new - m)
    l = jnp.sum(p, axis=-1, keepdims=True) + p_new
    pb = p.astype(BF16)
    half = s.shape[1] // 2
    inv_l = 1.0 / l
    o_ref[0, 0] = (_dot(pb[:, :half], latbf[:half, :]) + p_new * cn) * inv_l
    o_ref[1, 0] = _dot(pb[:, half:], latbf[half:, :]) * inv_l


def _decode_attention(page_table, ql, qp, lat_new, pe_new, cache_lat, cache_pet, layer):
    B, n_pages = page_table.shape
    H = MLA_HEADS
    ppc = 8
    grid_spec = pltpu.PrefetchScalarGridSpec(
        num_scalar_prefetch=1,
        grid=(B,),
        in_specs=[pl.BlockSpec((1, H, KV_RANK), lambda b, pt: (b, 0, 0)),
                  pl.BlockSpec((1, H, QK_ROPE), lambda b, pt: (b, 0, 0)),
                  pl.BlockSpec((1, 1, KV_RANK), lambda b, pt: (b, 0, 0)),
                  pl.BlockSpec((1, 1, QK_ROPE), lambda b, pt: (b, 0, 0)),
                  pl.BlockSpec(memory_space=pl.ANY),
                  pl.BlockSpec(memory_space=pl.ANY)],
        out_specs=pl.BlockSpec((2, 1, H, KV_RANK), lambda b, pt: (0, b, 0, 0)),
        scratch_shapes=[pltpu.VMEM((2, n_pages, PAGE_SIZE, KV_RANK), F32),
                        pltpu.VMEM((2, n_pages, QK_ROPE, PAGE_SIZE), F32),
                        pltpu.VMEM((n_pages * PAGE_SIZE, KV_RANK), BF16),
                        pltpu.VMEM((H, n_pages * PAGE_SIZE), F32),
                        pltpu.SemaphoreType.DMA((2, 2))],
    )
    return pl.pallas_call(
        functools.partial(_decode_kernel, layer=layer, n_pages=n_pages, pages_per_chunk=ppc),
        grid_spec=grid_spec,
        out_shape=jax.ShapeDtypeStruct((2, B, H, KV_RANK), F32),
        compiler_params=_cp("arbitrary"),
        name="mla_decode",
    )(page_table, ql.reshape(B, H, KV_RANK), qp.reshape(B, H, QK_ROPE),
      lat_new.reshape(B, 1, KV_RANK), pe_new.reshape(B, 1, QK_ROPE), cache_lat, cache_pet)


def _uv_kernel(ol_ref, wv_ref, o_ref):
    acc = jnp.zeros(o_ref.shape, F32)
    for h in range(MLA_HEADS):
        sl = slice(h * KV_RANK, (h + 1) * KV_RANK)
        acc = acc + _dot((ol_ref[0, :, sl] + ol_ref[1, :, sl]).astype(BF16), wv_ref[h])
    o_ref[...] = acc


def _uv_proj(olat, wvp):
    T = olat.shape[1]
    return pl.pallas_call(
        _uv_kernel,
        out_shape=jax.ShapeDtypeStruct((T, MLA_HEADS * V_HEAD), F32),
        compiler_params=pltpu.CompilerParams(vmem_limit_bytes=VMEM_LIMIT),
        name="mla_uv_sample",
    )(olat, wvp)


def _hgrn_step_kernel(hq_ref, hf_ref, hi_ref, hg_ref, lb_ref, ng_ref, s0_ref, o_ref, s_ref, *, bt):
    ng = ng_ref[...]
    for bi in range(bt):
        for h in range(HG_HEADS):
            sl = slice(h * HG_DK, (h + 1) * HG_DK)
            r = slice(bi, bi + 1)
            lbh = lb_ref[:, sl]
            f = lbh + (1.0 - lbh) * jax.nn.sigmoid(hf_ref[r, sl])
            kk = 1.0 - f
            q = _silu(hq_ref[r, sl]) * (HG_DK ** -0.5)
            v = hi_ref[r, sl]
            fcol = jnp.broadcast_to(f, (HG_DK, HG_DK)).T
            kcol = jnp.broadcast_to(kk, (HG_DK, HG_DK)).T
            s_new = fcol * s0_ref[0, bi, h] + kcol * v
            s_ref[bi, h] = s_new
            o = _dot(jnp.broadcast_to(q, (SUBLANES, HG_DK)).astype(BF16), s_new.astype(BF16))[0:1]
            o_ref[r, sl] = _rms(o, ng) * _silu(hg_ref[r, sl])


def _hgrn_step(z, lb, ng, states, layer):
    B = z.shape[0]
    bt = SUBLANES
    W = HG_WIDTH
    spec = lambda cb: pl.BlockSpec((bt, W), lambda i, cb=cb: (i, cb))
    blk = (bt, HG_HEADS, HG_DK, HG_DV)
    return pl.pallas_call(
        functools.partial(_hgrn_step_kernel, bt=bt),
        grid=(B // bt,),
        in_specs=[spec(0), spec(1), spec(2), spec(3),
                  pl.BlockSpec((1, W), lambda i: (0, 0)), pl.BlockSpec((1, HG_DV), lambda i: (0, 0)),
                  pl.BlockSpec((1,) + blk, lambda i: (layer, i, 0, 0, 0))],
        out_specs=[pl.BlockSpec((bt, W), lambda i: (i, 0)), pl.BlockSpec(blk, lambda i: (i, 0, 0, 0))],
        out_shape=[jax.ShapeDtypeStruct((B, W), F32), jax.ShapeDtypeStruct(states.shape[1:], F32)],
        compiler_params=_cp("parallel"),
        name="hgrn_sample",
    )(z, z, z, z, lb, ng, states)


def _conv_step_kernel(glu_ref, st_ref, w_ref, bias_ref, lg_ref, lb_ref, o_ref, ns_ref):
    K = CONV_WIDTH - 1
    g = glu_ref[...]
    u = g[:, :CONV_CH] * jax.nn.sigmoid(g[:, CONV_CH:])
    y = bias_ref[...] + w_ref[K:K + 1, :] * u
    for j in range(K):
        y = y + w_ref[j:j + 1, :] * st_ref[0, j]
    o_ref[...] = _silu(_layernorm(y, lg_ref[...], lb_ref[...]))
    for j in range(K - 1):
        ns_ref[j] = st_ref[0, j + 1]
    ns_ref[K - 1] = u


def _conv_step(z, states_t, w, bias, lg, lb, layer):
    B = z.shape[0]
    bt = min(B, 32)
    K = CONV_WIDTH - 1
    gblk = HG_SEG // GLU_SEG
    const = lambda i: (0, 0)
    return pl.pallas_call(
        _conv_step_kernel,
        grid=(B // bt,),
        in_specs=[pl.BlockSpec((bt, GLU_SEG), lambda i: (i, gblk)),
                  pl.BlockSpec((1, K, bt, CONV_CH), lambda i: (layer, 0, i, 0)),
                  pl.BlockSpec((CONV_WIDTH, CONV_CH), const), pl.BlockSpec((1, CONV_CH), const),
                  pl.BlockSpec((1, CONV_CH), const), pl.BlockSpec((1, CONV_CH), const)],
        out_specs=[pl.BlockSpec((bt, CONV_CH), lambda i: (i, 0)),
                   pl.BlockSpec((K, bt, CONV_CH), lambda i: (0, i, 0))],
        out_shape=[jax.ShapeDtypeStruct((B, CONV_CH), F32), jax.ShapeDtypeStruct((K, B, CONV_CH), F32)],
        compiler_params=_cp("parallel"),
        name="conv_sample",
    )(z, states_t, w, bias, lg, lb)


def _rope_tables(pos):
    half = QK_ROPE // 2
    inv = ROPE_THETA ** (-jnp.arange(half, dtype=F32) / half)
    ang = pos.astype(F32)[:, None] * inv[None, :]
    return jnp.cos(ang), jnp.sin(ang)


def _layer_params(l, w_in, mla_w_uq, mla_w_ukv):
    H = MLA_HEADS
    w = w_in[l]
    o_kv = Q_RANK
    o_pe = o_kv + KV_RANK
    o_hg = o_pe + QK_ROPE
    o_glu = o_hg + HG_SEG
    o_gate = o_glu + GLU_SEG
    kpe = w[:, o_pe:o_hg]
    half = QK_ROPE // 2
    w_za = w[:, o_hg:o_gate].astype(BF16)
    w_zb = jnp.concatenate(
        [w[:, o_gate:], w[:, :o_pe], kpe, kpe[:, half:], kpe[:, :half], jnp.zeros((D_MODEL, 64), F32)],
        axis=1).astype(BF16)

    uq = mla_w_uq[l].reshape(Q_RANK, H, QK_NOPE + QK_ROPE)
    nope, rope = uq[..., :QK_NOPE], uq[..., QK_NOPE:]
    rope_sw = jnp.concatenate([rope[..., half:], rope[..., :half]], axis=-1)
    z32 = jnp.zeros((Q_RANK, H, HEAD_PAD - QK_NOPE - QK_ROPE), F32)
    z64 = jnp.zeros((Q_RANK, H, QK_NOPE), F32)
    wqa = jnp.concatenate([nope, rope, z32], axis=-1).reshape(Q_RANK, H * HEAD_PAD).astype(BF16)
    wqb = jnp.concatenate([z64, rope_sw, z32], axis=-1).reshape(Q_RANK, H * HEAD_PAD).astype(BF16)
    wqn = jnp.concatenate([nope, z64], axis=-1).reshape(Q_RANK, H * HEAD_PAD).astype(BF16)
    wqr = rope.reshape(Q_RANK, H * QK_ROPE).astype(BF16)
    wqs = rope_sw.reshape(Q_RANK, H * QK_ROPE).astype(BF16)

    ukv = mla_w_ukv[l].reshape(KV_RANK, H, QK_NOPE + V_HEAD)
    uk, uv = ukv[..., :QK_NOPE], ukv[..., QK_NOPE:]
    wk = jnp.concatenate([uk, jnp.zeros((KV_RANK, H, HEAD_PAD - QK_NOPE), F32)], axis=-1)
    wk = wk.reshape(KV_RANK, H * HEAD_PAD).astype(BF16)
    wv = uv.reshape(KV_RANK, H * V_HEAD).T.astype(BF16)
    ukt = jnp.transpose(uk, (1, 2, 0))
    wukt = jnp.concatenate([ukt, jnp.zeros((H, HEAD_PAD - QK_NOPE, KV_RANK), F32)], axis=1).astype(BF16)
    head_of_col = jnp.arange(H * V_HEAD) // V_HEAD
    wvp = jnp.where(head_of_col[None, None, :] == jnp.arange(H)[:, None, None],
                    uv.reshape(KV_RANK, H * V_HEAD)[None], 0.0).astype(BF16)
    return dict(w_za=w_za, w_zb=w_zb, wqa=wqa, wqb=wqb, wqn=wqn, wqr=wqr, wqs=wqs, wk=wk, wv=wv, wukt=wukt, wvp=wvp)


def kernel(x_prompt, x_sample, cache_kv_latent, cache_k_rope, state_hgrn, state_conv, page_table,
           norm_ffa, w_ffa_gate, w_ffa_up, w_ffa_down, norm_mix, w_in,
           mla_q_norm, mla_w_uq, mla_kv_norm, mla_w_ukv, mla_w_o,
           hg_lower_bounds, hg_norm, hg_w_o,
           conv_w, conv_b, conv_ln_g, conv_ln_b, conv_w_o, w_out,
           norm_ffb, w_ffb_gate, w_ffb_up, w_ffb_down, norm_final):
    B, S, D = x_prompt.shape
    Bs, Ts, _ = x_sample.shape
    assert Ts == 1, "the sample kernels advance exactly one token per sequence"
    past_len = page_table.shape[1] * PAGE_SIZE
    H = MLA_HEADS
    TM = 1024 if (B * S) % 1024 == 0 else S
    TM_MIX = 512 if S % 512 == 0 else S
    TQ = 512 if S % 512 == 0 else S
    TC = 256

    lb_all = _lower_bounds(hg_lower_bounds)

    half = QK_ROPE // 2
    cos_p, sin_p = _rope_tables(jnp.arange(S))
    cos_s, sin_s = _rope_tables(past_len + jnp.arange(Ts))
    ones = jnp.ones((S, QK_NOPE), F32)
    zpad = jnp.zeros((S, HEAD_PAD - QK_NOPE - QK_ROPE), F32)
    cosq_p = jnp.concatenate([ones, cos_p, cos_p, zpad], axis=1) * (MLA_SCALE * LOG2E)
    sinq_p = jnp.concatenate([0.0 * ones, -sin_p, sin_p, zpad], axis=1) * (MLA_SCALE * LOG2E)
    cosk_p = jnp.concatenate([cos_p, cos_p], axis=1)
    sink_p = jnp.concatenate([-sin_p, sin_p], axis=1)
    cosk_s = jnp.concatenate([cos_s, cos_s], axis=1)
    sink_s = jnp.concatenate([-sin_s, sin_s], axis=1)
    cosq_s = jnp.tile(cosk_s, (1, H)) * MLA_SCALE
    sinq_s = jnp.tile(sink_s, (1, H)) * MLA_SCALE
    pk = (jnp.arange(H * HEAD_PAD)[None, :] % HEAD_PAD == QK_NOPE + jnp.arange(QK_ROPE)[:, None]).astype(BF16)

    cache_pet = jnp.swapaxes(cache_k_rope, 2, 3)
    state_conv_t = jnp.transpose(state_conv, (0, 2, 1, 3))

    row = lambda a, l: a[l][None, :]
    bf = lambda a, l: a[l].astype(BF16)
    za_outs = ((HG_SEG + GLU_SEG, F32),)
    zb_outs = ((GATE_SEG, BF16), (MLA_SEG, F32))
    n_ff = w_ffa_gate.shape[2] // FFN_CHUNK
    ff_in = lambda a, l: jnp.transpose(a[l].astype(BF16).reshape(D, n_ff, FFN_CHUNK), (1, 0, 2))
    ff_out = lambda a, l: a[l].astype(BF16).reshape(n_ff, FFN_CHUNK, D)

    xp = x_prompt.reshape(B * S, D)
    xs = x_sample.reshape(Bs * Ts, D)
    outs = [[] for _ in range(8)]
    for l in range(DEPTH):
        lp = _layer_params(l, w_in, mla_w_uq, mla_w_ukv)
        ffa = (row(norm_ffa, l), ff_in(w_ffa_gate, l), ff_in(w_ffa_up, l), ff_out(w_ffa_down, l))
        ffb = (row(norm_ffb, l), ff_in(w_ffb_gate, l), ff_in(w_ffb_up, l), ff_out(w_ffb_down, l))
        wo = (bf(mla_w_o, l), bf(hg_w_o, l), bf(conv_w_o, l), bf(w_out, l))
        lb = lb_all[l][None, :]
        ng = row(hg_norm, l)
        cvp = (conv_w[l], row(conv_b, l), row(conv_ln_g, l), row(conv_ln_b, l))

        xp = _ffn(xp, *ffa, tm=TM_MIX)
        za, = _inproj(xp, row(norm_mix, l), lp["w_za"], TM_MIX, za_outs)
        zb, zm = _inproj(xp, row(norm_mix, l), lp["w_zb"], TM_MIX, zb_outs)
        q, lat, pe, k, v = _mla_prep(zm, S, row(mla_q_norm, l), row(mla_kv_norm, l), lp["wqa"], lp["wqb"],
                                     cosq_p, sinq_p, lp["wk"], lp["wv"], pk, cosk_p, sink_p, tm=TM_MIX, tk=TQ)
        att = _attention(q, k, v, B, S, TQ)
        hg, hg_state = _hgrn(za, lb, ng, B, S)
        cv, cv_state = _conv(za, *cvp, B=B, S=S, tc=TC)
        xp = _post(xp, att, hg, cv, zb, *wo, tm=TM_MIX)
        xp = _ffn(xp, *ffb, tm=TM_MIX)
        outs[0].append(lat.reshape(B, S, KV_RANK))
        outs[1].append(pe.reshape(B, S, QK_ROPE))
        outs[2].append(hg_state)
        outs[3].append(cv_state)

        xs = _ffn(xs, *ffa, tm=Bs)
        zs, = _inproj(xs, row(norm_mix, l), lp["w_za"], Bs, za_outs)
        zsb, zsm = _inproj(xs, row(norm_mix, l), lp["w_zb"], Bs, zb_outs)
        ql, qp, lat_s, pe_s = _mla_prep_s(zsm, row(mla_q_norm, l), row(mla_kv_norm, l), lp["wqn"], lp["wqr"],
                                          lp["wqs"], lp["wukt"], cosq_s, sinq_s, cosk_s, sink_s)
        olat = _decode_attention(page_table, ql, qp, lat_s, pe_s, cache_kv_latent, cache_pet, l)
        att_s = _uv_proj(olat.reshape(2, Bs, H * KV_RANK), lp["wvp"])
        hg_s, hg_state_s = _hgrn_step(zs, lb, ng, state_hgrn, l)
        cv_s, cv_state_s = _conv_step(zs, state_conv_t, *cvp, layer=l)
        xs = _post(xs, att_s, hg_s, cv_s, zsb, *wo, tm=Bs)
        xs = _ffn(xs, *ffb, tm=Bs)
        outs[4].append(lat_s.reshape(Bs, Ts, KV_RANK))
        outs[5].append(pe_s.reshape(Bs, Ts, QK_ROPE))
        outs[6].append(hg_state_s)
        outs[7].append(cv_state_s)

    y_prompt = _final_norm(xp, norm_final[None, :], TM).reshape(B, S, D)
    y_sample = _final_norm(xs, norm_final[None, :], Bs).reshape(Bs, Ts, D)
    stacked = [jnp.stack(o) for o in outs]
    stacked[7] = jnp.transpose(stacked[7], (0, 2, 1, 3))
    return (y_prompt, y_sample) + tuple(stacked)
```
